```python
import math
import jax, jax.numpy as jnp
from jax import lax
import numpy as np

D_MODEL = 1024
BATCH = 32
SEQ = 2048
DEPTH = 2
DEC_BATCH = 128
DEC_SEQ = 4
PAST_LEN = 16384
PAGE_SIZE = 128

N_EVEN = (DEPTH + 1) // 2
N_ODD = DEPTH // 2
EPS = 1e-6
Q_BLOCK = 128
D_FF = 2 * D_MODEL
POOL_WINDOWS = (2, 4, 8, 16)
POOL_GROUPS = len(POOL_WINDOWS)
POOL_GROUP_DIM = D_MODEL // 8
POOL_DIM = POOL_GROUPS * POOL_GROUP_DIM
POOL_STATE = max(POOL_WINDOWS) - 1
MLA_HEADS = 8
MLA_NOPE = D_MODEL // 16
MLA_ROPE = D_MODEL // 32
MLA_V = D_MODEL // 16
MLA_Q_LORA = 3 * D_MODEL // 8
MLA_KV_LORA = D_MODEL // 4
MLA_SCALE = (MLA_NOPE + MLA_ROPE) ** -0.5
ROPE_THETA = 10000.0
DSA_HEADS = 16
DSA_KV_HEADS = 4
DSA_GROUP = DSA_HEADS // DSA_KV_HEADS
DSA_HEAD_DIM = D_MODEL // 16
DSA_SCALE = DSA_HEAD_DIM ** -0.5
IDX_HEADS = 8
IDX_DIM = D_MODEL // 16
IDX_SCALE = IDX_DIM ** -0.5
TOPK_MAX = 256

kernel_name = 'hybrid_pool_mla_dsa_macaron_step'


def rmsnorm(x, g):
    xf = x.astype(jnp.float32)
    xf = xf * lax.rsqrt(jnp.mean(xf * xf, axis=-1, keepdims=True) + EPS)
    return (xf * g.astype(jnp.float32)).astype(x.dtype)


def ffn_half(x, g, w1, w3, w2):
    h = rmsnorm(x, g)
    return x + 0.5 * ((jax.nn.silu(h @ w1) * (h @ w3)) @ w2)


def rope(x, pos):
    half = x.shape[-1] // 2
    inv = ROPE_THETA ** (-jnp.arange(half, dtype=jnp.float32) / half)
    ang = pos.astype(jnp.float32)[:, None] * inv[None, :]
    cos = jnp.cos(ang)[None, :, None, :]
    sin = jnp.sin(ang)[None, :, None, :]
    xf = x.astype(jnp.float32)
    x1, x2 = xf[..., :half], xf[..., half:]
    return jnp.concatenate([x1 * cos - x2 * sin, x2 * cos + x1 * sin], axis=-1).astype(x.dtype)


def pool_mix(u, hist, pos, w_pool, scale):
    B, T, _ = u.shape
    ext = jnp.concatenate([hist, u], axis=1)
    cs = jnp.cumsum(ext.astype(jnp.float32), axis=1)
    cs = jnp.concatenate([jnp.zeros((B, 1, POOL_DIM), jnp.float32), cs], axis=1)
    end = cs[:, POOL_STATE + 1:POOL_STATE + 1 + T]
    means = []
    for g, w in enumerate(POOL_WINDOWS):
        sl = slice(g * POOL_GROUP_DIM, (g + 1) * POOL_GROUP_DIM)
        start = cs[:, POOL_STATE + 1 - w:POOL_STATE + 1 - w + T, sl]
        cnt = jnp.minimum(pos + 1, w).astype(jnp.float32)[None, :, None]
        means.append((end[..., sl] - start) / cnt)
    pooled = (jnp.concatenate(means, axis=-1) - u.astype(jnp.float32)).astype(u.dtype)
    pooled = pooled.reshape(B, T, POOL_GROUPS, POOL_GROUP_DIM)
    y = jnp.einsum('btgc,gcd->btgd', pooled, w_pool).reshape(B, T, POOL_DIM) * scale
    return y, ext[:, -POOL_STATE:]


def even_project(xn, pos, w_in, g_q, g_kv, w_uq, w_uk):
    B, T, _ = xn.shape
    o1 = POOL_DIM
    o2 = o1 + MLA_Q_LORA
    o3 = o2 + MLA_KV_LORA
    z = xn @ w_in
    u = z[..., :o1]
    cq = rmsnorm(z[..., o1:o2], g_q)
    ckv = rmsnorm(z[..., o2:o3], g_kv)
    kr = rope(z[:, :, None, o3:], pos)[:, :, 0, :]
    q = (cq @ w_uq).reshape(B, T, MLA_HEADS, MLA_NOPE + MLA_ROPE)
    q_rope = rope(q[..., MLA_NOPE:], pos)
    q_lat = jnp.einsum('bthn,chn->bthc', q[..., :MLA_NOPE], w_uk)
    return u, q_lat, q_rope, ckv, kr


def mla_scores(q_lat, q_rope, ckv, kr):
    s = jnp.einsum('bqhc,bkc->bhqk', q_lat, ckv, preferred_element_type=jnp.float32)
    s = s + jnp.einsum('bqhr,bkr->bhqk', q_rope, kr, preferred_element_type=jnp.float32)
    return s * MLA_SCALE


def mla_prompt(q_lat, q_rope, ckv, kr):
    B, T = q_lat.shape[:2]
    kpos = jnp.arange(T)

    def block(i):
        start = i * Q_BLOCK
        ql = lax.dynamic_slice_in_dim(q_lat, start, Q_BLOCK, axis=1)
        qr = lax.dynamic_slice_in_dim(q_rope, start, Q_BLOCK, axis=1)
        qpos = start + jnp.arange(Q_BLOCK)
        s = jnp.where(kpos[None, :] <= qpos[:, None], mla_scores(ql, qr, ckv, kr), -jnp.inf)
        p = jax.nn.softmax(s, axis=-1).astype(ckv.dtype)
        return jnp.einsum('bhqk,bkc->bqhc', p, ckv)

    o = lax.map(block, jnp.arange(T // Q_BLOCK))
    return jnp.moveaxis(o, 0, 1).reshape(B, T, MLA_HEADS, MLA_KV_LORA)


def mla_sample(q_lat, q_rope, ckv, kr, ckv_pool, kr_pool, layer, page_table):
    Bd, T = q_lat.shape[:2]
    P = page_table.shape[1] * PAGE_SIZE
    ckv_past = ckv_pool[layer, page_table].reshape(Bd, P, MLA_KV_LORA)
    kr_past = kr_pool[layer, page_table].reshape(Bd, P, MLA_ROPE)
    s_past = mla_scores(q_lat, q_rope, ckv_past, kr_past)
    tri = jnp.tril(jnp.ones((T, T), dtype=bool))
    s_new = jnp.where(tri, mla_scores(q_lat, q_rope, ckv, kr), -jnp.inf)
    p = jax.nn.softmax(jnp.concatenate([s_past, s_new], axis=-1), axis=-1).astype(ckv.dtype)
    return (jnp.einsum('bhqk,bkc->bqhc', p[..., :P], ckv_past)
            + jnp.einsum('bhqk,bkc->bqhc', p[..., P:], ckv))


def even_merge(y_pool, o_lat, w_uv, w_out):
    B, T = y_pool.shape[:2]
    o = jnp.einsum('bthc,chv->bthv', o_lat, w_uv).reshape(B, T, MLA_HEADS * MLA_V)
    return jnp.concatenate([y_pool, o], axis=-1) @ w_out


def odd_project(xn, w_in):
    B, T, _ = xn.shape
    o1 = DSA_HEADS * DSA_HEAD_DIM
    o2 = o1 + DSA_KV_HEADS * DSA_HEAD_DIM
    o3 = o2 + DSA_KV_HEADS * DSA_HEAD_DIM
    o4 = o3 + IDX_HEADS * IDX_DIM
    o5 = o4 + IDX_DIM
    z = xn @ w_in
    q = z[..., :o1].reshape(B, T, DSA_KV_HEADS, DSA_GROUP, DSA_HEAD_DIM)
    k = z[..., o1:o2].reshape(B, T, DSA_KV_HEADS, DSA_HEAD_DIM)
    v = z[..., o2:o3].reshape(B, T, DSA_KV_HEADS, DSA_HEAD_DIM)
    qi = z[..., o3:o4].reshape(B, T, IDX_HEADS, IDX_DIM)
    ki = z[..., o4:o5]
    wi = z[..., o5:] * (IDX_HEADS ** -0.5)
    return q, k, v, qi, ki, wi


def index_scores(qi, wi, ki):
    s = jnp.einsum('bqhd,bkd->bqhk', qi, ki, preferred_element_type=jnp.float32) * IDX_SCALE
    return jnp.einsum('bqhk,bqh->bqk', jax.nn.relu(s), wi.astype(jnp.float32))


def sparse_attend(q, ks, vs, valid):
    s = jnp.einsum('bqngd,bqknd->bqngk', q, ks, preferred_element_type=jnp.float32) * DSA_SCALE
    s = jnp.where(valid[:, :, None, None, :], s, -jnp.inf)
    p = jax.nn.softmax(s, axis=-1).astype(vs.dtype)
    o = jnp.einsum('bqngk,bqknd->bqngd', p, vs)
    return o.reshape(o.shape[0], o.shape[1], DSA_HEADS * DSA_HEAD_DIM)


def dsa_prompt(q, k, v, qi, ki, wi):
    B, T = q.shape[:2]
    topk = min(TOPK_MAX, T // 4)
    kpos = jnp.arange(T)
    take = jax.vmap(lambda rows, idx: rows[idx])

    def block(i):
        start = i * Q_BLOCK
        qb = lax.dynamic_slice_in_dim(q, start, Q_BLOCK, axis=1)
        qib = lax.dynamic_slice_in_dim(qi, start, Q_BLOCK, axis=1)
        wib = lax.dynamic_slice_in_dim(wi, start, Q_BLOCK, axis=1)
        qpos = start + jnp.arange(Q_BLOCK)
        isc = jnp.where(kpos[None, :] <= qpos[:, None], index_scores(qib, wib, ki), -jnp.inf)
        _, idx = lax.top_k(isc, topk)
        valid = idx <= qpos[None, :, None]
        return sparse_attend(qb, take(k, idx), take(v, idx), valid)

    o = lax.map(block, jnp.arange(T // Q_BLOCK))
    return jnp.moveaxis(o, 0, 1).reshape(B, T, DSA_HEADS * DSA_HEAD_DIM)


def dsa_sample(q, k, v, qi, ki, wi, k_pool, v_pool, ki_pool, layer, page_table):
    Bd, T = q.shape[:2]
    P = page_table.shape[1] * PAGE_SIZE
    topk = min(TOPK_MAX, (P + T) // 4)
    ki_past = ki_pool[layer, page_table].reshape(Bd, P, IDX_DIM)
    tri = jnp.tril(jnp.ones((T, T), dtype=bool))
    isc = jnp.concatenate([index_scores(qi, wi, ki_past),
                           jnp.where(tri, index_scores(qi, wi, ki), -jnp.inf)], axis=-1)
    _, idx = lax.top_k(isc, topk)
    valid = idx <= (P + jnp.arange(T))[None, :, None]
    pidx = jnp.minimum(idx, P - 1)
    page = jnp.take_along_axis(page_table, (pidx // PAGE_SIZE).reshape(Bd, -1), axis=1).reshape(pidx.shape)
    rows = page * PAGE_SIZE + pidx % PAGE_SIZE
    k_rows = k_pool.reshape(k_pool.shape[0], -1, DSA_KV_HEADS, DSA_HEAD_DIM)
    v_rows = v_pool.reshape(v_pool.shape[0], -1, DSA_KV_HEADS, DSA_HEAD_DIM)
    nidx = jnp.clip(idx - P, 0, T - 1)
    take = jax.vmap(lambda r, i: r[i])
    from_past = (idx < P)[..., None, None]
    ks = jnp.where(from_past, k_rows[layer, rows], take(k, nidx))
    vs = jnp.where(from_past, v_rows[layer, rows], take(v, nidx))
    return sparse_attend(q, ks, vs, valid)


def setup_inputs(seed: int = 0) -> dict:
    key = jax.random.key(seed)
    keys = iter(jax.random.split(key, 32))
    f32 = jnp.float32
    n_pages = PAST_LEN // PAGE_SIZE
    n_phys = (5 * DEC_BATCH * n_pages + 3) // 4

    def normal(shape, scale=1.0):
        return jax.random.normal(next(keys), shape, f32) * scale

    def gain(shape):
        return 1.0 + 0.02 * jax.random.normal(next(keys), shape, f32)

    even_in = POOL_DIM + MLA_Q_LORA + MLA_KV_LORA + MLA_ROPE
    odd_in = (DSA_HEADS + 2 * DSA_KV_HEADS) * DSA_HEAD_DIM + IDX_HEADS * IDX_DIM + IDX_DIM + IDX_HEADS
    mla_out = POOL_DIM + MLA_HEADS * MLA_V
    perm = jax.random.permutation(next(keys), n_phys)[: DEC_BATCH * n_pages]
    page_table = perm.reshape(DEC_BATCH, n_pages).astype(jnp.int32)
    return {
        'x_prompt': normal((BATCH, SEQ, D_MODEL)),
        'x_sample': normal((DEC_BATCH, DEC_SEQ, D_MODEL)),
        'cache_mla_ckv': normal((N_EVEN, n_phys, PAGE_SIZE, MLA_KV_LORA)),
        'cache_mla_kr': normal((N_EVEN, n_phys, PAGE_SIZE, MLA_ROPE)),
        'cache_dsa_k': normal((N_ODD, n_phys, PAGE_SIZE, DSA_KV_HEADS, DSA_HEAD_DIM)),
        'cache_dsa_v': normal((N_ODD, n_phys, PAGE_SIZE, DSA_KV_HEADS, DSA_HEAD_DIM)),
        'cache_dsa_kidx': normal((N_ODD, n_phys, PAGE_SIZE, IDX_DIM)),
        'state_pool': normal((N_EVEN, DEC_BATCH, POOL_STATE, POOL_DIM)),
        'page_table': page_table,
        'ffn_norm': gain((DEPTH, 2, D_MODEL)),
        'ffn_w1': normal((DEPTH, 2, D_MODEL, D_FF), D_MODEL ** -0.5),
        'ffn_w3': normal((DEPTH, 2, D_MODEL, D_FF), D_MODEL ** -0.5),
        'ffn_w2': normal((DEPTH, 2, D_FF, D_MODEL), D_FF ** -0.5),
        'mix_norm': gain((DEPTH, D_MODEL)),
        'final_norm': gain((D_MODEL,)),
        'even_w_in': normal((N_EVEN, D_MODEL, even_in), D_MODEL ** -0.5),
        'mla_q_norm': gain((N_EVEN, MLA_Q_LORA)),
        'mla_kv_norm': gain((N_EVEN, MLA_KV_LORA)),
        'mla_w_uq': normal((N_EVEN, MLA_Q_LORA, MLA_HEADS * (MLA_NOPE + MLA_ROPE)), MLA_Q_LORA ** -0.5),
        'mla_w_uk': normal((N_EVEN, MLA_KV_LORA, MLA_HEADS, MLA_NOPE), MLA_KV_LORA ** -0.5),
        'mla_w_uv': normal((N_EVEN, MLA_KV_LORA, MLA_HEADS, MLA_V), MLA_KV_LORA ** -0.5),
        'pool_w': normal((N_EVEN, POOL_GROUPS, POOL_GROUP_DIM, POOL_GROUP_DIM), POOL_GROUP_DIM ** -0.5),
        'pool_scale': 1.0 + 0.1 * normal((N_EVEN, POOL_DIM)),
        'even_w_out': normal((N_EVEN, mla_out, D_MODEL), mla_out ** -0.5),
        'odd_w_in': normal((N_ODD, D_MODEL, odd_in), D_MODEL ** -0.5),
        'odd_w_out': normal((N_ODD, DSA_HEADS * DSA_HEAD_DIM, D_MODEL), (DSA_HEADS * DSA_HEAD_DIM) ** -0.5),
    }


def reference(x_prompt, x_sample, cache_mla_ckv, cache_mla_kr, cache_dsa_k, cache_dsa_v,
              cache_dsa_kidx, state_pool, page_table,
              ffn_norm, ffn_w1, ffn_w3, ffn_w2, mix_norm, final_norm,
              even_w_in, mla_q_norm, mla_kv_norm, mla_w_uq, mla_w_uk, mla_w_uv,
              pool_w, pool_scale, even_w_out, odd_w_in, odd_w_out):
    past_len = page_table.shape[1] * PAGE_SIZE
    pos_p = jnp.arange(x_prompt.shape[1], dtype=jnp.int32)
    pos_s = past_len + jnp.arange(x_sample.shape[1], dtype=jnp.int32)
    xp, xs = x_prompt, x_sample
    p_ckv, p_kr, p_k, p_v, p_ki, p_pool = [], [], [], [], [], []
    s_ckv, s_kr, s_k, s_v, s_ki, s_pool = [], [], [], [], [], []
    for l in range(DEPTH):
        j = l // 2
        xp = ffn_half(xp, ffn_norm[l, 0], ffn_w1[l, 0], ffn_w3[l, 0], ffn_w2[l, 0])
        xs = ffn_half(xs, ffn_norm[l, 0], ffn_w1[l, 0], ffn_w3[l, 0], ffn_w2[l, 0])
        hp = rmsnorm(xp, mix_norm[l])
        hs = rmsnorm(xs, mix_norm[l])
        if l % 2 == 0:
            ew = (even_w_in[j], mla_q_norm[j], mla_kv_norm[j], mla_w_uq[j], mla_w_uk[j])
            u, q_lat, q_rope, ckv, kr = even_project(hp, pos_p, *ew)
            hist0 = jnp.zeros((u.shape[0], POOL_STATE, POOL_DIM), u.dtype)
            y_pool, hist = pool_mix(u, hist0, pos_p, pool_w[j], pool_scale[j])
            o_lat = mla_prompt(q_lat, q_rope, ckv, kr)
            xp = xp + even_merge(y_pool, o_lat, mla_w_uv[j], even_w_out[j])
            p_ckv.append(ckv)
            p_kr.append(kr)
            p_pool.append(hist)
            u, q_lat, q_rope, ckv, kr = even_project(hs, pos_s, *ew)
            y_pool, hist = pool_mix(u, state_pool[j], pos_s, pool_w[j], pool_scale[j])
            o_lat = mla_sample(q_lat, q_rope, ckv, kr, cache_mla_ckv, cache_mla_kr, j, page_table)
            xs = xs + even_merge(y_pool, o_lat, mla_w_uv[j], even_w_out[j])
            s_ckv.append(ckv)
            s_kr.append(kr)
            s_pool.append(hist)
        else:
            q, k, v, qi, ki, wi = odd_project(hp, odd_w_in[j])
            xp = xp + dsa_prompt(q, k, v, qi, ki, wi) @ odd_w_out[j]
            p_k.append(k)
            p_v.append(v)
            p_ki.append(ki)
            q, k, v, qi, ki, wi = odd_project(hs, odd_w_in[j])
            xs = xs + dsa_sample(q, k, v, qi, ki, wi, cache_dsa_k, cache_dsa_v, cache_dsa_kidx,
                                 j, page_table) @ odd_w_out[j]
            s_k.append(k)
            s_v.append(v)
            s_ki.append(ki)
        xp = ffn_half(xp, ffn_norm[l, 1], ffn_w1[l, 1], ffn_w3[l, 1], ffn_w2[l, 1])
        xs = ffn_half(xs, ffn_norm[l, 1], ffn_w1[l, 1], ffn_w3[l, 1], ffn_w2[l, 1])
    y_prompt = rmsnorm(xp, final_norm)
    y_sample = rmsnorm(xs, final_norm)
    return (y_prompt, y_sample,
            jnp.stack(p_ckv), jnp.stack(p_kr), jnp.stack(p_k), jnp.stack(p_v), jnp.stack(p_ki), jnp.stack(p_pool),
            jnp.stack(s_ckv), jnp.stack(s_kr), jnp.stack(s_k), jnp.stack(s_v), jnp.stack(s_ki), jnp.stack(s_pool))
```

```python
import functools

import numpy as np
import jax
import jax.numpy as jnp
from jax import lax
from jax.experimental import pallas as pl
from jax.experimental.pallas import tpu as pltpu

D_MODEL = 1024
D_FF = 2 * D_MODEL
EPS = 1e-6
PAGE_SIZE = 128
POOL_WINDOWS = (2, 4, 8, 16)
POOL_GROUPS = len(POOL_WINDOWS)
POOL_GROUP_DIM = D_MODEL // 8
POOL_DIM = POOL_GROUPS * POOL_GROUP_DIM
POOL_STATE = max(POOL_WINDOWS) - 1
MLA_HEADS = 8
MLA_NOPE = D_MODEL // 16
MLA_ROPE = D_MODEL // 32
MLA_V = D_MODEL // 16
MLA_Q_LORA = 3 * D_MODEL // 8
MLA_KV_LORA = D_MODEL // 4
MLA_SCALE = (MLA_NOPE + MLA_ROPE) ** -0.5
ROPE_THETA = 10000.0
DSA_HEADS = 16
DSA_KV_HEADS = 4
DSA_GROUP = DSA_HEADS // DSA_KV_HEADS
DSA_HEAD_DIM = D_MODEL // 16
DSA_SCALE = DSA_HEAD_DIM ** -0.5
IDX_HEADS = 8
IDX_DIM = D_MODEL // 16
IDX_SCALE = IDX_DIM ** -0.5
TOPK_MAX = 256

LANES = 128
SUBLANES = 8
VMEM_LIMIT = 56 * 1024 * 1024

TOKEN_TILE = 512
FF_CHUNK = 512
ATTN_TILE = 256
PAGES_PER_STEP = 16

_CD = jnp.bfloat16
_F32 = jnp.float32
_NT = (((1,), (1,)), ((), ()))
_MIN_I32 = np.int32(-2 ** 31)
_MASKED = -1e30
_NEG_INF = float("-inf")


def _cparams(n_grid):
    return pltpu.CompilerParams(dimension_semantics=("arbitrary",) * n_grid, vmem_limit_bytes=VMEM_LIMIT)


def _dot(a, b):
    return jnp.dot(a, b, preferred_element_type=_F32)


def _dot_nt(a, b):
    return lax.dot_general(a, b, _NT, preferred_element_type=_F32)


def _rms(x, g):
    return x * lax.rsqrt(jnp.mean(x * x, axis=-1, keepdims=True) + EPS) * g


def _sort_key(x):
    bits = lax.bitcast_convert_type(x, jnp.int32)
    bits = jnp.where(bits == _MIN_I32, 0, bits)
    return jnp.where(bits < 0, bits ^ np.int32(0x7FFFFFFF), bits)


def _tile(n, cap):
    t = min(cap, n)
    assert n % t == 0, (n, t)
    return t


def _ffn_body(*refs, n_pre, final):
    x_ref, g_ref, w1_ref, w3_ref, w2_ref = refs[:5]
    pre = refs[5:5 + 2 * n_pre]
    rest = refs[5 + 2 * n_pre:]
    o_ref = rest[-1]
    x = x_ref[...]
    if n_pre:
        d = _dot(pre[0][...], pre[1][...])
        for k in range(1, n_pre):
            d = d + _dot(pre[2 * k][...], pre[2 * k + 1][...])
        x = x + d
    h = _rms(x, g_ref[...]).astype(_CD)
    acc = jnp.zeros_like(x)
    for c in range(D_FF // FF_CHUNK):
        sl = slice(c * FF_CHUNK, (c + 1) * FF_CHUNK)
        a = _dot(h, w1_ref[:, sl])
        b = _dot(h, w3_ref[:, sl])
        acc = acc + _dot((a * jax.nn.sigmoid(a) * b).astype(_CD), w2_ref[sl, :])
    y = x + 0.5 * acc
    if final:
        y = _rms(y, rest[0][...])
    o_ref[...] = y


def _ffn(x, g, w1, w3, w2, pre=(), final_g=None):
    n, d = x.shape
    tm = _tile(n, TOKEN_TILE)
    const = lambda i: (0, 0)
    row = lambda i: (i, 0)
    in_specs = [pl.BlockSpec((tm, d), row), pl.BlockSpec((1, d), const),
                pl.BlockSpec(w1.shape, const), pl.BlockSpec(w3.shape, const), pl.BlockSpec(w2.shape, const)]
    args = [x, g.reshape(1, d), w1, w3, w2]
    for a, w in pre:
        in_specs += [pl.BlockSpec((tm, a.shape[1]), row), pl.BlockSpec(w.shape, const)]
        args += [a, w]
    if final_g is not None:
        in_specs.append(pl.BlockSpec((1, d), const))
        args.append(final_g.reshape(1, d))
    return pl.pallas_call(
        functools.partial(_ffn_body, n_pre=len(pre), final=final_g is not None),
        grid=(n // tm,), in_specs=in_specs, out_specs=pl.BlockSpec((tm, d), row),
        out_shape=jax.ShapeDtypeStruct((n, d), _F32), compiler_params=_cparams(1), name="ffn_half",
    )(*args)


_E_U, _E_CQ, _E_CKV, _E_KR, _E_KRS, _E_END = 0, 512, 896, 1152, 1280, 1408
_Q_NOPE, _Q_ROPE, _Q_ROPES, _Q_END = 0, 512, 768, 1024
_QR = MLA_HEADS * MLA_ROPE


def _even_proj_body(x_ref, g_ref, w_ref, gq_ref, gkv_ref, wuq_ref, cos_ref, sin_ref,
                    u_ref, qn_ref, qr_ref, ckv_ref, kr_ref, ckvb_ref, krb_ref):
    h = _rms(x_ref[...], g_ref[...]).astype(_CD)
    z = _dot(h, w_ref[...])
    u_ref[...] = z[:, _E_U:_E_CQ]
    cq = _rms(z[:, _E_CQ:_E_CKV], gq_ref[...]).astype(_CD)
    ckv = _rms(z[:, _E_CKV:_E_KR], gkv_ref[...])
    cos = cos_ref[...]
    sin = sin_ref[...]
    kr = z[:, _E_KR:_E_KR + MLA_ROPE] * cos[:, :MLA_ROPE] + z[:, _E_KRS:_E_KRS + MLA_ROPE] * sin[:, :MLA_ROPE]
    q = _dot(cq, wuq_ref[...])
    qn_ref[...] = q[:, _Q_NOPE:_Q_ROPE].astype(_CD)
    qr_ref[...] = (q[:, _Q_ROPE:_Q_ROPES] * cos + q[:, _Q_ROPES:_Q_END] * sin).astype(_CD)
    ckv_ref[...] = ckv
    kr_ref[...] = kr
    ckvb_ref[...] = ckv.astype(_CD)
    krb_ref[...] = kr.astype(_CD)


def _even_proj(x, g, w_all, gq, gkv, wuq_all, cos_t, sin_t):
    n, d = x.shape
    tm = _tile(n, TOKEN_TILE)
    nblk = cos_t.shape[0] // tm
    const = lambda i: (0, 0)
    row = lambda i: (i, 0)
    tab = lambda i: (i % nblk, 0)
    widths = (POOL_DIM, MLA_HEADS * MLA_NOPE, _QR, MLA_KV_LORA, MLA_ROPE, MLA_KV_LORA, MLA_ROPE)
    dtypes = (_F32, _CD, _CD, _F32, _F32, _CD, _CD)
    return pl.pallas_call(
        _even_proj_body, grid=(n // tm,),
        in_specs=[pl.BlockSpec((tm, d), row), pl.BlockSpec((1, d), const), pl.BlockSpec(w_all.shape, const),
                  pl.BlockSpec((1, MLA_Q_LORA), const), pl.BlockSpec((1, MLA_KV_LORA), const),
                  pl.BlockSpec(wuq_all.shape, const), pl.BlockSpec((tm, _QR), tab), pl.BlockSpec((tm, _QR), tab)],
        out_specs=[pl.BlockSpec((tm, w), row) for w in widths],
        out_shape=[jax.ShapeDtypeStruct((n, w), dt) for w, dt in zip(widths, dtypes)],
        compiler_params=_cparams(1), name="even_proj",
    )(x, g.reshape(1, d), w_all, gq.reshape(1, -1), gkv.reshape(1, -1), wuq_all, cos_t, sin_t)


_HIST_PAD = 2 * SUBLANES


def _pool_prompt_body(u_ref, w_ref, sc_ref, y_ref, ext_ref, *, T):
    ext_ref[0:_HIST_PAD, :] = jnp.zeros((_HIST_PAD, POOL_DIM), _F32)
    ext_ref[_HIST_PAD:_HIST_PAD + T, :] = u_ref[...]
    pos = lax.broadcasted_iota(jnp.int32, (T, 1), 0)
    for g, w in enumerate(POOL_WINDOWS):
        sl = slice(g * POOL_GROUP_DIM, (g + 1) * POOL_GROUP_DIM)
        s = ext_ref[_HIST_PAD:_HIST_PAD + T, sl]
        for j in range(1, w):
            s = s + ext_ref[_HIST_PAD - j:_HIST_PAD - j + T, sl]
        cnt = jnp.minimum(pos + 1, w).astype(_F32)
        pooled = (s / cnt - u_ref[:, sl]).astype(_CD)
        y_ref[:, sl] = (_dot(pooled, w_ref[g]) * sc_ref[:, sl]).astype(y_ref.dtype)


def _pool_prompt(u, w_pool, scale, B, T):
    return pl.pallas_call(
        functools.partial(_pool_prompt_body, T=T), grid=(B,),
        in_specs=[pl.BlockSpec((T, POOL_DIM), lambda b: (b, 0)),
                  pl.BlockSpec(w_pool.shape, lambda b: (0, 0, 0)), pl.BlockSpec((1, POOL_DIM), lambda b: (0, 0))],
        out_specs=pl.BlockSpec((T, POOL_DIM), lambda b: (b, 0)),
        out_shape=jax.ShapeDtypeStruct((B * T, POOL_DIM), _CD),
        scratch_shapes=[pltpu.VMEM((_HIST_PAD + T, POOL_DIM), _F32)],
        compiler_params=_cparams(1), name="pool_prompt",
    )(u, w_pool, scale.reshape(1, POOL_DIM))


def _pool_sample_body(ext_ref, w_ref, sc_ref, y_ref, *, Td, pos0):
    for t in range(Td):
        for g, w in enumerate(POOL_WINDOWS):
            sl = slice(g * POOL_GROUP_DIM, (g + 1) * POOL_GROUP_DIM)
            s = ext_ref[POOL_STATE + t, :, sl]
            for j in range(1, w):
                s = s + ext_ref[POOL_STATE + t - j, :, sl]
            cnt = float(min(pos0 + t + 1, w))
            pooled = (s / cnt - ext_ref[POOL_STATE + t, :, sl]).astype(_CD)
            y_ref[t, :, sl] = (_dot(pooled, w_ref[g]) * sc_ref[:, sl]).astype(y_ref.dtype)


def _pool_sample(ext, w_pool, scale, Td, pos0):
    _, Bd, _ = ext.shape
    return pl.pallas_call(
        functools.partial(_pool_sample_body, Td=Td, pos0=pos0),
        out_shape=jax.ShapeDtypeStruct((Td, Bd, POOL_DIM), _CD),
        compiler_params=pltpu.CompilerParams(vmem_limit_bytes=VMEM_LIMIT), name="pool_sample",
    )(ext, w_pool, scale.reshape(1, POOL_DIM))


def _softmax_step(s, vals, m_ref, l_ref, acc_ref):
    m_prev = m_ref[...]
    m_new = jnp.maximum(m_prev, jnp.max(s, axis=-1, keepdims=True))
    alpha = jnp.exp(m_prev - m_new)
    p = jnp.exp(s - m_new)
    l_ref[...] = alpha * l_ref[...] + jnp.sum(p, axis=-1, keepdims=True)
    acc_ref[...] = alpha * acc_ref[...] + _dot(p.astype(_CD), vals)
    m_ref[...] = m_new


def _mla_prompt_body(qn_ref, qr_ref, ckv_ref, kr_ref, wuk_ref, wuv_ref, o_ref, m_ref, l_ref, acc_ref, *, TQ):
    i = pl.program_id(1)
    tri = lax.broadcasted_iota(jnp.int32, (TQ, TQ), 1) <= lax.broadcasted_iota(jnp.int32, (TQ, TQ), 0)
    for h in range(MLA_HEADS):
        ql = _dot(qn_ref[:, h * MLA_NOPE:(h + 1) * MLA_NOPE], wuk_ref[h]).astype(_CD)
        qr = qr_ref[:, h * MLA_ROPE:(h + 1) * MLA_ROPE]

        def scores(off):
            kc = ckv_ref[pl.ds(off, TQ), :]
            s = _dot_nt(ql, kc) + _dot_nt(qr, kr_ref[pl.ds(off, TQ), :])
            return s * MLA_SCALE, kc

        s, kc = scores(pl.multiple_of(i * TQ, TQ))
        s = jnp.where(tri, s, _NEG_INF)
        m0 = jnp.max(s, axis=-1, keepdims=True)
        p = jnp.exp(s - m0)
        m_ref[...] = m0
        l_ref[...] = jnp.sum(p, axis=-1, keepdims=True)
        acc_ref[...] = _dot(p.astype(_CD), kc)

        def body(j, c):
            s, kc = scores(pl.multiple_of(j * TQ, TQ))
            _softmax_step(s, kc, m_ref, l_ref, acc_ref)
            return c

        lax.fori_loop(0, i, body, 0)
        o_lat = (acc_ref[...] / l_ref[...]).astype(_CD)
        o_ref[:, h * MLA_V:(h + 1) * MLA_V] = _dot(o_lat, wuv_ref[h]).astype(o_ref.dtype)


def _mla_prompt(qn, qr, ckvb, krb, wuk, wuv, B, T):
    tq = _tile(T, ATTN_TILE)
    nq = T // tq
    qmap = lambda b, i: (b * nq + i, 0)
    smap = lambda b, i: (b, 0)
    c3 = lambda b, i: (0, 0, 0)
    return pl.pallas_call(
        functools.partial(_mla_prompt_body, TQ=tq), grid=(B, nq),
        in_specs=[pl.BlockSpec((tq, qn.shape[1]), qmap), pl.BlockSpec((tq, qr.shape[1]), qmap),
                  pl.BlockSpec((T, MLA_KV_LORA), smap), pl.BlockSpec((T, MLA_ROPE), smap),
                  pl.BlockSpec(wuk.shape, c3), pl.BlockSpec(wuv.shape, c3)],
        out_specs=pl.BlockSpec((tq, MLA_HEADS * MLA_V), qmap),
        out_shape=jax.ShapeDtypeStruct((B * T, MLA_HEADS * MLA_V), _CD),
        scratch_shapes=[pltpu.VMEM((tq, 1), _F32), pltpu.VMEM((tq, 1), _F32), pltpu.VMEM((tq, MLA_KV_LORA), _F32)],
        compiler_params=_cparams(2), name="mla_prompt",
    )(qn, qr, ckvb, krb, wuk, wuv)


def _mla_sample_body(pt_ref, qn_ref, qr_ref, nckv_ref, nkr_ref, wuk_ref, wuv_ref, *rest, PG, Td, TdP):
    ckv_pages, kr_pages = rest[:PG], rest[PG:2 * PG]
    o_ref, ql_s, qr_s, m_ref, l_ref, acc_ref = rest[2 * PG:]
    i = pl.program_id(1)
    R = MLA_HEADS * TdP

    @pl.when(i == 0)
    def _():
        for h in range(MLA_HEADS):
            ql_s[h * TdP:(h + 1) * TdP, :] = _dot(qn_ref[0][:, h * MLA_NOPE:(h + 1) * MLA_NOPE], wuk_ref[h])
            qr_s[h * TdP:(h + 1) * TdP, :] = qr_ref[0][:, h * MLA_ROPE:(h + 1) * MLA_ROPE].astype(_F32)
        m_ref[...] = jnp.full((R, 1), _NEG_INF, _F32)
        l_ref[...] = jnp.zeros((R, 1), _F32)
        acc_ref[...] = jnp.zeros((R, MLA_KV_LORA), _F32)

    ql = ql_s[...].astype(_CD)
    qr = qr_s[...].astype(_CD)
    for j in range(PG):
        kc = ckv_pages[j][...].astype(_CD)
        s = (_dot_nt(ql, kc) + _dot_nt(qr, kr_pages[j][...].astype(_CD))) * MLA_SCALE
        _softmax_step(s, kc, m_ref, l_ref, acc_ref)

    @pl.when(i == pl.num_programs(1) - 1)
    def _():
        kc = nckv_ref[0]
        s = (_dot_nt(ql, kc) + _dot_nt(qr, nkr_ref[0])) * MLA_SCALE
        t = lax.broadcasted_iota(jnp.int32, (R, PAGE_SIZE), 0) % TdP
        c = lax.broadcasted_iota(jnp.int32, (R, PAGE_SIZE), 1)
        s = jnp.where((c <= t) & (c < Td), s, _NEG_INF)
        _softmax_step(s, kc, m_ref, l_ref, acc_ref)
        o_lat = (acc_ref[...] / l_ref[...]).astype(_CD)
        for h in range(MLA_HEADS):
            o_ref[0, :, h * MLA_V:(h + 1) * MLA_V] = _dot(o_lat[h * TdP:(h + 1) * TdP], wuv_ref[h])


def _page_spec(block, layer, PG, j):
    nd = len(block)

    def index_map(b, i, pt):
        return (layer, pt[b, i * PG + j]) + (0,) * (nd - 2)

    return pl.BlockSpec(block, index_map)


def _mla_sample(page_table, qn, qr, nckv, nkr, wuk, wuv, cache_ckv, cache_kr, layer, Td):
    Bd, TdP, _ = qn.shape
    n_pages = page_table.shape[1]
    pg = _tile(n_pages, PAGES_PER_STEP)
    seq = lambda b, i, pt: (b, 0, 0)
    c3 = lambda b, i, pt: (0, 0, 0)
    R = MLA_HEADS * TdP
    in_specs = [pl.BlockSpec((1, TdP, qn.shape[2]), seq), pl.BlockSpec((1, TdP, qr.shape[2]), seq),
                pl.BlockSpec((1, PAGE_SIZE, MLA_KV_LORA), seq), pl.BlockSpec((1, PAGE_SIZE, MLA_ROPE), seq),
                pl.BlockSpec(wuk.shape, c3), pl.BlockSpec(wuv.shape, c3)]
    in_specs += [_page_spec((None, None, PAGE_SIZE, MLA_KV_LORA), layer, pg, j) for j in range(pg)]
    in_specs += [_page_spec((None, None, PAGE_SIZE, MLA_ROPE), layer, pg, j) for j in range(pg)]
    return pl.pallas_call(
        functools.partial(_mla_sample_body, PG=pg, Td=Td, TdP=TdP),
        grid_spec=pltpu.PrefetchScalarGridSpec(
            num_scalar_prefetch=1, grid=(Bd, n_pages // pg), in_specs=in_specs,
            out_specs=pl.BlockSpec((1, TdP, MLA_HEADS * MLA_V), seq),
            scratch_shapes=[pltpu.VMEM((R, MLA_KV_LORA), _F32), pltpu.VMEM((R, MLA_ROPE), _F32),
                            pltpu.VMEM((R, 1), _F32), pltpu.VMEM((R, 1), _F32), pltpu.VMEM((R, MLA_KV_LORA), _F32)]),
        out_shape=jax.ShapeDtypeStruct((Bd, TdP, MLA_HEADS * MLA_V), _F32),
        compiler_params=_cparams(2), name="mla_sample",
    )(page_table, qn, qr, nckv, nkr, wuk, wuv, *([cache_ckv] * pg), *([cache_kr] * pg))


_O_Q, _O_K, _O_V, _O_QI, _O_KI, _O_WI, _O_END = 0, 1024, 1280, 1536, 2048, 2176, 2304
_KV = DSA_KV_HEADS * DSA_HEAD_DIM


def _odd_proj_body(x_ref, g_ref, w_ref, q_ref, k_ref, v_ref, qi_ref, ki_ref, wi_ref, kb_ref, vb_ref, kib_ref):
    h = _rms(x_ref[...], g_ref[...]).astype(_CD)
    z = _dot(h, w_ref[...])
    q_ref[...] = z[:, _O_Q:_O_K].astype(_CD)
    k = z[:, _O_K:_O_V]
    v = z[:, _O_V:_O_QI]
    ki = z[:, _O_KI:_O_KI + IDX_DIM]
    k_ref[...] = k
    v_ref[...] = v
    qi_ref[...] = z[:, _O_QI:_O_KI].astype(_CD)
    ki_ref[...] = ki
    wi_ref[...] = z[:, _O_WI:_O_WI + IDX_HEADS] * (IDX_HEADS ** -0.5)
    kb_ref[...] = k.astype(_CD)
    vb_ref[...] = v.astype(_CD)
    kib_ref[...] = ki.astype(_CD)


def _odd_proj(x, g, w_all):
    n, d = x.shape
    tm = _tile(n, TOKEN_TILE)
    const = lambda i: (0, 0)
    row = lambda i: (i, 0)
    widths = (DSA_HEADS * DSA_HEAD_DIM, _KV, _KV, IDX_HEADS * IDX_DIM, IDX_DIM, IDX_HEADS, _KV, _KV, IDX_DIM)
    dtypes = (_CD, _F32, _F32, _CD, _F32, _F32, _CD, _CD, _CD)
    return pl.pallas_call(
        _odd_proj_body, grid=(n // tm,),
        in_specs=[pl.BlockSpec((tm, d), row), pl.BlockSpec((1, d), const), pl.BlockSpec(w_all.shape, const)],
        out_specs=[pl.BlockSpec((tm, w), row) for w in widths],
        out_shape=[jax.ShapeDtypeStruct((n, w), dt) for w, dt in zip(widths, dtypes)],
        compiler_params=_cparams(1), name="odd_proj",
    )(x, g.reshape(1, d), w_all)


def _kth_largest_search(count, topk, idx_bits, shape):
    def value_bit(b, t_u):
        cand_u = t_u | lax.shift_left(jnp.int32(1), 31 - b)
        cand = cand_u ^ _MIN_I32
        return jnp.where(count(lambda k, idx: k >= cand) >= topk, cand_u, t_u)

    thr = lax.fori_loop(0, 32, value_bit, jnp.zeros(shape, jnp.int32)) ^ _MIN_I32
    need = topk - count(lambda k, idx: k > thr)

    def index_bit(b, cut):
        cand = cut | lax.shift_left(jnp.int32(1), idx_bits - 1 - b)
        return jnp.where(count(lambda k, idx: (k == thr) & (idx < cand)) < need, cand, cut)

    cut = lax.fori_loop(0, idx_bits, index_bit, jnp.zeros(shape, jnp.int32))
    return thr, cut


def _dsa_prompt_body(q_ref, qi_ref, wi_ref, k_ref, v_ref, ki_ref, o_ref, key_s, bias_s, m_ref, l_ref, acc_ref,
                     *, TQ, T, topk):
    i = pl.program_id(1)
    nch = i + 1
    rowpos = i * TQ + lax.broadcasted_iota(jnp.int32, (TQ, 1), 0)
    col0 = lax.broadcasted_iota(jnp.int32, (1, TQ), 1)
    wi = wi_ref[...]

    def index_scores(j, c):
        off = pl.multiple_of(j * TQ, TQ)
        kic = ki_ref[pl.ds(off, TQ), :]
        isc = jnp.zeros((TQ, TQ), _F32)
        for h in range(IDX_HEADS):
            s = _dot_nt(qi_ref[:, h * IDX_DIM:(h + 1) * IDX_DIM], kic) * IDX_SCALE
            isc = isc + jnp.maximum(s, 0.0) * wi[:, h:h + 1]
        isc = jnp.where(off + col0 <= rowpos, isc, _NEG_INF)
        key_s[:, pl.ds(off, TQ)] = _sort_key(isc)
        return c

    lax.fori_loop(0, nch, index_scores, 0)

    def count(pred):
        def body(j, acc):
            off = pl.multiple_of(j * TQ, TQ)
            hit = pred(key_s[:, pl.ds(off, TQ)], off + col0)
            return acc + jnp.sum(hit.astype(_F32), axis=1, keepdims=True)
        return lax.fori_loop(0, nch, body, jnp.zeros((TQ, 1), _F32))

    thr, cut = _kth_largest_search(count, float(topk), max(1, (T - 1).bit_length()), (TQ, 1))

    def selection_bias(j, c):
        off = pl.multiple_of(j * TQ, TQ)
        kk = key_s[:, pl.ds(off, TQ)]
        idx = off + col0
        sel = ((kk > thr) | ((kk == thr) & (idx <= cut))) & (idx <= rowpos)
        bias_s[:, pl.ds(off, TQ)] = jnp.where(sel, 0.0, _MASKED)
        return c

    lax.fori_loop(0, nch, selection_bias, 0)

    for hh in range(DSA_HEADS):
        n = hh // DSA_GROUP
        qh = q_ref[:, hh * DSA_HEAD_DIM:(hh + 1) * DSA_HEAD_DIM]
        m_ref[...] = jnp.full((TQ, 1), _MASKED, _F32)
        l_ref[...] = jnp.zeros((TQ, 1), _F32)
        acc_ref[...] = jnp.zeros((TQ, DSA_HEAD_DIM), _F32)

        def body(j, c):
            off = pl.multiple_of(j * TQ, TQ)
            kc = k_ref[pl.ds(off, TQ), n * DSA_HEAD_DIM:(n + 1) * DSA_HEAD_DIM]
            vc = v_ref[pl.ds(off, TQ), n * DSA_HEAD_DIM:(n + 1) * DSA_HEAD_DIM]
            s = _dot_nt(qh, kc) * DSA_SCALE + bias_s[:, pl.ds(off, TQ)]
            _softmax_step(s, vc, m_ref, l_ref, acc_ref)
            return c

        lax.fori_loop(0, nch, body, 0)
        o_ref[:, hh * DSA_HEAD_DIM:(hh + 1) * DSA_HEAD_DIM] = (acc_ref[...] / l_ref[...]).astype(o_ref.dtype)


def _dsa_prompt(q, qi, wi, kb, vb, kib, B, T):
    tq = _tile(T, ATTN_TILE)
    nq = T // tq
    topk = min(TOPK_MAX, T // 4)
    qmap = lambda b, i: (b * nq + i, 0)
    smap = lambda b, i: (b, 0)
    return pl.pallas_call(
        functools.partial(_dsa_prompt_body, TQ=tq, T=T, topk=topk), grid=(B, nq),
        in_specs=[pl.BlockSpec((tq, q.shape[1]), qmap), pl.BlockSpec((tq, qi.shape[1]), qmap),
                  pl.BlockSpec((tq, IDX_HEADS), qmap),
                  pl.BlockSpec((T, _KV), smap), pl.BlockSpec((T, _KV), smap), pl.BlockSpec((T, IDX_DIM), smap)],
        out_specs=pl.BlockSpec((tq, DSA_HEADS * DSA_HEAD_DIM), qmap),
        out_shape=jax.ShapeDtypeStruct((B * T, DSA_HEADS * DSA_HEAD_DIM), _CD),
        scratch_shapes=[pltpu.VMEM((tq, T), jnp.int32), pltpu.VMEM((tq, T), _F32),
                        pltpu.VMEM((tq, 1), _F32), pltpu.VMEM((tq, 1), _F32), pltpu.VMEM((tq, DSA_HEAD_DIM), _F32)],
        compiler_params=_cparams(2), name="dsa_prompt",
    )(q, qi, wi, kb, vb, kib)


def _dsa_select_body(pt_ref, qi_ref, wi_ref, nki_ref, *rest, PG, P, Td, TdP, topk):
    ki_pages = rest[:PG]
    key_ref, thr_ref, cut_ref, qs, ws = rest[PG:]
    i = pl.program_id(1)
    R = IDX_HEADS * TdP

    @pl.when(i == 0)
    def _():
        for h in range(IDX_HEADS):
            qs[h * TdP:(h + 1) * TdP, :] = qi_ref[0][:, h * IDX_DIM:(h + 1) * IDX_DIM].astype(_F32)
            ws[h * TdP:(h + 1) * TdP, :] = jnp.broadcast_to(wi_ref[0][:, h:h + 1], (TdP, LANES))

    def index_scores(kic):
        s = jnp.maximum(_dot_nt(qs[...].astype(_CD), kic) * IDX_SCALE, 0.0) * ws[...]
        isc = s[0:TdP]
        for h in range(1, IDX_HEADS):
            isc = isc + s[h * TdP:(h + 1) * TdP]
        return isc

    for j in range(PG):
        off = pl.multiple_of((i * PG + j) * PAGE_SIZE, PAGE_SIZE)
        key_ref[0, :, pl.ds(off, PAGE_SIZE)] = _sort_key(index_scores(ki_pages[j][...].astype(_CD)))

    @pl.when(i == pl.num_programs(1) - 1)
    def _():
        isc = index_scores(nki_ref[0])
        t = lax.broadcasted_iota(jnp.int32, (TdP, PAGE_SIZE), 0)
        c = lax.broadcasted_iota(jnp.int32, (TdP, PAGE_SIZE), 1)
        key_ref[0, :, P:P + PAGE_SIZE] = _sort_key(jnp.where((c <= t) & (c < Td), isc, _NEG_INF))
        idx = lax.broadcasted_iota(jnp.int32, (1, P + PAGE_SIZE), 1)

        def count(pred):
            return jnp.sum(pred(key_ref[0], idx).astype(_F32), axis=1, keepdims=True)

        thr, cut = _kth_largest_search(count, float(topk), (P + PAGE_SIZE - 1).bit_length(), (TdP, 1))
        thr_ref[0] = jnp.broadcast_to(thr, (TdP, LANES))
        cut_ref[0] = jnp.broadcast_to(cut, (TdP, LANES))


def _dsa_select(page_table, qi, wi, nki, cache_kidx, layer, Td):
    Bd, TdP, _ = qi.shape
    n_pages = page_table.shape[1]
    P = n_pages * PAGE_SIZE
    pg = _tile(n_pages, PAGES_PER_STEP)
    topk = min(TOPK_MAX, (P + Td) // 4)
    seq = lambda b, i, pt: (b, 0, 0)
    R = IDX_HEADS * TdP
    in_specs = [pl.BlockSpec((1, TdP, qi.shape[2]), seq), pl.BlockSpec((1, TdP, IDX_HEADS), seq),
                pl.BlockSpec((1, PAGE_SIZE, IDX_DIM), seq)]
    in_specs += [_page_spec((None, None, PAGE_SIZE, IDX_DIM), layer, pg, j) for j in range(pg)]
    return pl.pallas_call(
        functools.partial(_dsa_select_body, PG=pg, P=P, Td=Td, TdP=TdP, topk=topk),
        grid_spec=pltpu.PrefetchScalarGridSpec(
            num_scalar_prefetch=1, grid=(Bd, n_pages // pg), in_specs=in_specs,
            out_specs=[pl.BlockSpec((1, TdP, P + PAGE_SIZE), seq), pl.BlockSpec((1, TdP, LANES), seq),
                       pl.BlockSpec((1, TdP, LANES), seq)],
            scratch_shapes=[pltpu.VMEM((R, IDX_DIM), _F32), pltpu.VMEM((R, LANES), _F32)]),
        out_shape=[jax.ShapeDtypeStruct((Bd, TdP, P + PAGE_SIZE), jnp.int32),
                   jax.ShapeDtypeStruct((Bd, TdP, LANES), jnp.int32), jax.ShapeDtypeStruct((Bd, TdP, LANES), jnp.int32)],
        compiler_params=_cparams(2), name="dsa_select",
    )(page_table, qi, wi, nki, *([cache_kidx] * pg))


def _dsa_sample_body(pt_ref, q_ref, key_ref, nkey_ref, thr_ref, cut_ref, nk_ref, nv_ref, *rest, PG, P, Td, TdP):
    k_pages, v_pages = rest[:PG], rest[PG:2 * PG]
    o_ref, qs, m_ref, l_ref, acc_ref = rest[2 * PG:]
    i = pl.program_id(1)
    R = DSA_GROUP * TdP

    @pl.when(i == 0)
    def _():
        for hh in range(DSA_HEADS):
            n, g = divmod(hh, DSA_GROUP)
            qs[n, g * TdP:(g + 1) * TdP, :] = q_ref[0][:, hh * DSA_HEAD_DIM:(hh + 1) * DSA_HEAD_DIM].astype(_F32)
        m_ref[...] = jnp.full((DSA_KV_HEADS, R, 1), _MASKED, _F32)
        l_ref[...] = jnp.zeros((DSA_KV_HEADS, R, 1), _F32)
        acc_ref[...] = jnp.zeros((DSA_KV_HEADS, R, DSA_HEAD_DIM), _F32)

    thr = thr_ref[0][:, 0:1]
    cut = cut_ref[0][:, 0:1]
    t = lax.broadcasted_iota(jnp.int32, (TdP, PAGE_SIZE), 0)
    c = lax.broadcasted_iota(jnp.int32, (TdP, PAGE_SIZE), 1)

    def attend(kk, idx, visible, k_of, v_of):
        sel = ((kk > thr) | ((kk == thr) & (idx <= cut))) & visible
        bias = jnp.where(sel, 0.0, _MASKED)
        bias = jnp.concatenate([bias] * DSA_GROUP, axis=0)
        for n in range(DSA_KV_HEADS):
            s = _dot_nt(qs[n].astype(_CD), k_of(n)) * DSA_SCALE + bias
            _softmax_step(s, v_of(n), m_ref.at[n], l_ref.at[n], acc_ref.at[n])

    for j in range(PG):
        base = (i * PG + j) * PAGE_SIZE
        attend(key_ref[0][:, j * PAGE_SIZE:(j + 1) * PAGE_SIZE], base + c, True,
               lambda n: k_pages[j][:, n, :].astype(_CD), lambda n: v_pages[j][:, n, :].astype(_CD))

    @pl.when(i == pl.num_programs(1) - 1)
    def _():
        attend(nkey_ref[0], P + c, (c <= t) & (c < Td),
               lambda n: nk_ref[0][:, n * DSA_HEAD_DIM:(n + 1) * DSA_HEAD_DIM],
               lambda n: nv_ref[0][:, n * DSA_HEAD_DIM:(n + 1) * DSA_HEAD_DIM])
        for hh in range(DSA_HEADS):
            n, g = divmod(hh, DSA_GROUP)
            o = acc_ref[n, g * TdP:(g + 1) * TdP, :] / l_ref[n, g * TdP:(g + 1) * TdP, :]
            o_ref[0, :, hh * DSA_HEAD_DIM:(hh + 1) * DSA_HEAD_DIM] = o


def _dsa_sample(page_table, q, keys, nkeys, thr, cut, nk, nv, cache_k, cache_v, layer, Td):
    Bd, TdP, _ = q.shape
    n_pages = page_table.shape[1]
    P = n_pages * PAGE_SIZE
    pg = _tile(n_pages, PAGES_PER_STEP)
    seq = lambda b, i, pt: (b, 0, 0)
    R = DSA_GROUP * TdP
    in_specs = [pl.BlockSpec((1, TdP, q.shape[2]), seq),
                pl.BlockSpec((1, TdP, pg * PAGE_SIZE), lambda b, i, pt: (b, 0, i)),
                pl.BlockSpec((1, TdP, PAGE_SIZE), seq), pl.BlockSpec((1, TdP, LANES), seq),
                pl.BlockSpec((1, TdP, LANES), seq),
                pl.BlockSpec((1, PAGE_SIZE, _KV), seq), pl.BlockSpec((1, PAGE_SIZE, _KV), seq)]
    page_block = (None, None, PAGE_SIZE, DSA_KV_HEADS, DSA_HEAD_DIM)
    in_specs += [_page_spec(page_block, layer, pg, j) for j in range(pg)]
    in_specs += [_page_spec(page_block, layer, pg, j) for j in range(pg)]
    return pl.pallas_call(
        functools.partial(_dsa_sample_body, PG=pg, P=P, Td=Td, TdP=TdP),
        grid_spec=pltpu.PrefetchScalarGridSpec(
            num_scalar_prefetch=1, grid=(Bd, n_pages // pg), in_specs=in_specs,
            out_specs=pl.BlockSpec((1, TdP, DSA_HEADS * DSA_HEAD_DIM), seq),
            scratch_shapes=[pltpu.VMEM((DSA_KV_HEADS, R, DSA_HEAD_DIM), _F32), pltpu.VMEM((DSA_KV_HEADS, R, 1), _F32),
                            pltpu.VMEM((DSA_KV_HEADS, R, 1), _F32), pltpu.VMEM((DSA_KV_HEADS, R, DSA_HEAD_DIM), _F32)]),
        out_shape=jax.ShapeDtypeStruct((Bd, TdP, DSA_HEADS * DSA_HEAD_DIM), _F32),
        compiler_params=_cparams(2), name="dsa_sample",
    )(page_table, q, keys, nkeys, thr, cut, nk, nv, *([cache_k] * pg), *([cache_v] * pg))


def _rope_tables(pos):
    half = MLA_ROPE // 2
    inv = ROPE_THETA ** (-jnp.arange(half, dtype=_F32) / half)
    ang = pos.astype(_F32)[:, None] * inv[None, :]
    cos, sin = jnp.cos(ang), jnp.sin(ang)
    return (jnp.tile(jnp.concatenate([cos, cos], axis=1), (1, MLA_HEADS)),
            jnp.tile(jnp.concatenate([-sin, sin], axis=1), (1, MLA_HEADS)))


def _pack_cols(w, pieces, total):
    out = jnp.zeros((w.shape[0], total), w.dtype)
    for start, cols in pieces:
        out = out.at[:, start:start + len(cols)].set(w[:, np.asarray(cols)])
    return out.astype(_CD)


def _swap_halves(n):
    return (np.arange(n) + n // 2) % n


def _pad_rows(a, rows):
    return jnp.pad(a, ((0, 0), (0, rows - a.shape[1]), (0, 0)))


def kernel(x_prompt, x_sample, cache_mla_ckv, cache_mla_kr, cache_dsa_k, cache_dsa_v, cache_dsa_kidx, state_pool, page_table, ffn_norm, ffn_w1, ffn_w3, ffn_w2, mix_norm, final_norm, even_w_in, mla_q_norm, mla_kv_norm, mla_w_uq, mla_w_uk, mla_w_uv, pool_w, pool_scale, even_w_out, odd_w_in, odd_w_out):
    B, T, D = x_prompt.shape
    Bd, Td, _ = x_sample.shape
    depth = ffn_norm.shape[0]
    P = page_table.shape[1] * PAGE_SIZE
    TdP = -(-Td // SUBLANES) * SUBLANES
    assert D == D_MODEL and Td <= PAGE_SIZE and T >= POOL_STATE
    Np, Ns = B * T, Bd * Td
    tm_s = _tile(Ns, TOKEN_TILE)
    assert tm_s % Td == 0

    pos_p = jnp.arange(T, dtype=jnp.int32)
    pos_s = P + jnp.arange(Td, dtype=jnp.int32)
    cos_p, sin_p = _rope_tables(pos_p)
    cos_s, sin_s = (jnp.tile(a, (tm_s // Td, 1)) for a in _rope_tables(pos_s))

    def seq_rows(a, rows=TdP):
        return _pad_rows(a.reshape(Bd, Td, a.shape[1]), rows)

    xp = x_prompt.reshape(Np, D)
    xs = x_sample.reshape(Ns, D)
    pre_p, pre_s = (), ()
    outs = {k: [] for k in ("p_ckv", "p_kr", "p_k", "p_v", "p_ki", "p_pool", "s_ckv", "s_kr", "s_k", "s_v", "s_ki", "s_pool")}
    for l in range(depth):
        j = l // 2
        wa = [a.astype(_CD) for a in (ffn_w1[l, 0], ffn_w3[l, 0], ffn_w2[l, 0])]
        wb = [a.astype(_CD) for a in (ffn_w1[l, 1], ffn_w3[l, 1], ffn_w2[l, 1])]
        xp = _ffn(xp, ffn_norm[l, 0], *wa, pre=pre_p)
        xs = _ffn(xs, ffn_norm[l, 0], *wa, pre=pre_s)
        if l % 2 == 0:
            kr0 = POOL_DIM + MLA_Q_LORA + MLA_KV_LORA
            w_all = _pack_cols(even_w_in[j], [(_E_U, np.arange(kr0)), (_E_KR, kr0 + np.arange(MLA_ROPE)),
                                              (_E_KRS, kr0 + _swap_halves(MLA_ROPE))], _E_END)
            per_head = MLA_NOPE + MLA_ROPE
            heads = np.arange(MLA_HEADS)[:, None] * per_head
            wuq_all = _pack_cols(mla_w_uq[j], [(_Q_NOPE, (heads + np.arange(MLA_NOPE)[None]).ravel()),
                                               (_Q_ROPE, (heads + MLA_NOPE + np.arange(MLA_ROPE)[None]).ravel()),
                                               (_Q_ROPES, (heads + MLA_NOPE + _swap_halves(MLA_ROPE)[None]).ravel())], _Q_END)
            wuk = jnp.transpose(mla_w_uk[j], (1, 2, 0)).astype(_CD)
            wuv = jnp.transpose(mla_w_uv[j], (1, 0, 2)).astype(_CD)
            w_out = even_w_out[j].astype(_CD)
            pw = pool_w[j].astype(_CD)
            u, qn, qr, ckv, kr, ckvb, krb = _even_proj(xp, mix_norm[l], w_all, mla_q_norm[j], mla_kv_norm[j], wuq_all, cos_p, sin_p)
            y_pool = _pool_prompt(u, pw, pool_scale[j], B, T)
            o = _mla_prompt(qn, qr, ckvb, krb, wuk, wuv, B, T)
            pre_p = ((y_pool, w_out[:POOL_DIM]), (o, w_out[POOL_DIM:]))
            outs["p_ckv"].append(ckv.reshape(B, T, MLA_KV_LORA))
            outs["p_kr"].append(kr.reshape(B, T, MLA_ROPE))
            outs["p_pool"].append(u.reshape(B, T, POOL_DIM)[:, T - POOL_STATE:])
            u, qn, qr, ckv, kr, ckvb, krb = _even_proj(xs, mix_norm[l], w_all, mla_q_norm[j], mla_kv_norm[j], wuq_all, cos_s, sin_s)
            ext = jnp.concatenate([state_pool[j], u.reshape(Bd, Td, POOL_DIM)], axis=1)
            y_pool = _pool_sample(jnp.swapaxes(ext, 0, 1), pw, pool_scale[j], Td, P)
            y_pool = jnp.swapaxes(y_pool, 0, 1).reshape(Ns, POOL_DIM)
            o = _mla_sample(page_table, seq_rows(qn), seq_rows(qr), seq_rows(ckvb, PAGE_SIZE), seq_rows(krb, PAGE_SIZE),
                            wuk, wuv, cache_mla_ckv, cache_mla_kr, j, Td)
            o = o[:, :Td].reshape(Ns, MLA_HEADS * MLA_V).astype(_CD)
            pre_s = ((y_pool, w_out[:POOL_DIM]), (o, w_out[POOL_DIM:]))
            outs["s_ckv"].append(ckv.reshape(Bd, Td, MLA_KV_LORA))
            outs["s_kr"].append(kr.reshape(Bd, Td, MLA_ROPE))
            outs["s_pool"].append(ext[:, -POOL_STATE:])
        else:
            o5 = (DSA_HEADS + 2 * DSA_KV_HEADS) * DSA_HEAD_DIM + IDX_HEADS * IDX_DIM + IDX_DIM
            o4 = o5 - IDX_DIM
            w_all = _pack_cols(odd_w_in[j], [(_O_Q, np.arange(o4)), (_O_KI, o4 + np.arange(IDX_DIM)),
                                             (_O_WI, o5 + np.arange(IDX_HEADS))], _O_END)
            w_out = odd_w_out[j].astype(_CD)
            q, k, v, qi, ki, wi, kb, vb, kib = _odd_proj(xp, mix_norm[l], w_all)
            o = _dsa_prompt(q, qi, wi, kb, vb, kib, B, T)
            pre_p = ((o, w_out),)
            outs["p_k"].append(k.reshape(B, T, DSA_KV_HEADS, DSA_HEAD_DIM))
            outs["p_v"].append(v.reshape(B, T, DSA_KV_HEADS, DSA_HEAD_DIM))
            outs["p_ki"].append(ki.reshape(B, T, IDX_DIM))
            q, k, v, qi, ki, wi, kb, vb, kib = _odd_proj(xs, mix_norm[l], w_all)
            keys, thr, cut = _dsa_select(page_table, seq_rows(qi), seq_rows(wi), seq_rows(kib, PAGE_SIZE),
                                         cache_dsa_kidx, j, Td)
            o = _dsa_sample(page_table, seq_rows(q), keys, keys[:, :, P:], thr, cut, seq_rows(kb, PAGE_SIZE),
                            seq_rows(vb, PAGE_SIZE), cache_dsa_k, cache_dsa_v, j, Td)
            o = o[:, :Td].reshape(Ns, DSA_HEADS * DSA_HEAD_DIM).astype(_CD)
            pre_s = ((o, w_out),)
            outs["s_k"].append(k.reshape(Bd, Td, DSA_KV_HEADS, DSA_HEAD_DIM))
            outs["s_v"].append(v.reshape(Bd, Td, DSA_KV_HEADS, DSA_HEAD_DIM))
            outs["s_ki"].append(ki.reshape(Bd, Td, IDX_DIM))
        last = l == depth - 1
        xp = _ffn(xp, ffn_norm[l, 1], *wb, pre=pre_p, final_g=final_norm if last else None)
        xs = _ffn(xs, ffn_norm[l, 1], *wb, pre=pre_s, final_g=final_norm if last else None)
        pre_p, pre_s = (), ()
    st = {k: jnp.stack(v) for k, v in outs.items()}
    return (xp.reshape(B, T, D), xs.reshape(Bd, Td, D),
            st["p_ckv"], st["p_kr"], st["p_k"], st["p_v"], st["p_ki"], st["p_pool"],
            st["s_ckv"], st["s_kr"], st["s_k"], st["s_v"], st["s_ki"], st["s_pool"])
```

```python
import functools

import numpy as np
import jax
import jax.numpy as jnp
from jax import lax
from jax.experimental import pallas as pl
from jax.experimental.pallas import tpu as pltpu

D_MODEL = 1024
D_FF = 2 * D_MODEL
EPS = 1e-6
PAGE_SIZE = 128
POOL_WINDOWS = (2, 4, 8, 16)
POOL_GROUPS = len(POOL_WINDOWS)
POOL_GROUP_DIM = D_MODEL // 8
POOL_DIM = POOL_GROUPS * POOL_GROUP_DIM
POOL_STATE = max(POOL_WINDOWS) - 1
MLA_HEADS = 8
MLA_NOPE = D_MODEL // 16
MLA_ROPE = D_MODEL // 32
MLA_V = D_MODEL // 16
MLA_Q_LORA = 3 * D_MODEL // 8
MLA_KV_LORA = D_MODEL // 4
MLA_SCALE = (MLA_NOPE + MLA_ROPE) ** -0.5
ROPE_THETA = 10000.0
DSA_HEADS = 16
DSA_KV_HEADS = 4
DSA_GROUP = DSA_HEADS // DSA_KV_HEADS
DSA_HEAD_DIM = D_MODEL // 16
DSA_SCALE = DSA_HEAD_DIM ** -0.5
IDX_HEADS = 8
IDX_DIM = D_MODEL // 16
IDX_SCALE = IDX_DIM ** -0.5
TOPK_MAX = 256

LANES = 128
SUBLANES = 8
VMEM_LIMIT = 56 * 1024 * 1024

TOKEN_TILE = 512
FF_CHUNK = 512
ATTN_TILE = 256
PAGES_PER_STEP = 16

_CD = jnp.bfloat16
_F32 = jnp.float32
_NT = (((1,), (1,)), ((), ()))
_MIN_I32 = np.int32(-2 ** 31)
_MASKED = -1e30
_NEG_INF = float("-inf")


def _cparams(n_grid):
    return pltpu.CompilerParams(dimension_semantics=("arbitrary",) * n_grid, vmem_limit_bytes=VMEM_LIMIT)


def _dot(a, b):
    return jnp.dot(a, b, preferred_element_type=_F32)


def _dot_nt(a, b):
    return lax.dot_general(a, b, _NT, preferred_element_type=_F32)


def _rms(x, g):
    return x * lax.rsqrt(jnp.mean(x * x, axis=-1, keepdims=True) + EPS) * g


def _sort_key(x):
    bits = lax.bitcast_convert_type(x, jnp.int32)
    bits = jnp.where(bits == _MIN_I32, 0, bits)
    return jnp.where(bits < 0, bits ^ np.int32(0x7FFFFFFF), bits)


def _tile(n, cap):
    t = min(cap, n)
    assert n % t == 0, (n, t)
    return t


def _ffn_body(*refs, n_pre, final):
    x_ref, g_ref, w1_ref, w3_ref, w2_ref = refs[:5]
    pre = refs[5:5 + 2 * n_pre]
    rest = refs[5 + 2 * n_pre:]
    o_ref = rest[-1]
    x = x_ref[...]
    if n_pre:
        d = _dot(pre[0][...], pre[1][...])
        for k in range(1, n_pre):
            d = d + _dot(pre[2 * k][...], pre[2 * k + 1][...])
        x = x + d
    h = _rms(x, g_ref[...]).astype(_CD)
    acc = jnp.zeros_like(x)
    for c in range(D_FF // FF_CHUNK):
        sl = slice(c * FF_CHUNK, (c + 1) * FF_CHUNK)
        a = _dot(h, w1_ref[:, sl])
        b = _dot(h, w3_ref[:, sl])
        acc = acc + _dot((a * jax.nn.sigmoid(a) * b).astype(_CD), w2_ref[sl, :])
    y = x + 0.5 * acc
    if final:
        y = _rms(y, rest[0][...])
    o_ref[...] = y


def _ffn(x, g, w1, w3, w2, pre=(), final_g=None):
    n, d = x.shape
    tm = _tile(n, TOKEN_TILE)
    const = lambda i: (0, 0)
    row = lambda i: (i, 0)
    in_specs = [pl.BlockSpec((tm, d), row), pl.BlockSpec((1, d), const),
                pl.BlockSpec(w1.shape, const), pl.BlockSpec(w3.shape, const), pl.BlockSpec(w2.shape, const)]
    args = [x, g.reshape(1, d), w1, w3, w2]
    for a, w in pre:
        in_specs += [pl.BlockSpec((tm, a.shape[1]), row), pl.BlockSpec(w.shape, const)]
        args += [a, w]
    if final_g is not None:
        in_specs.append(pl.BlockSpec((1, d), const))
        args.append(final_g.reshape(1, d))
    return pl.pallas_call(
        functools.partial(_ffn_body, n_pre=len(pre), final=final_g is not None),
        grid=(n // tm,), in_specs=in_specs, out_specs=pl.BlockSpec((tm, d), row),
        out_shape=jax.ShapeDtypeStruct((n, d), _F32), compiler_params=_cparams(1), name="ffn_half",
    )(*args)


_E_U, _E_CQ, _E_CKV, _E_KR, _E_KRS, _E_END = 0, 512, 896, 1152, 1280, 1408
_Q_NOPE, _Q_ROPE, _Q_ROPES, _Q_END = 0, 512, 768, 1024
_QR = MLA_HEADS * MLA_ROPE


def _even_proj_body(x_ref, g_ref, w_ref, gq_ref, gkv_ref, wuq_ref, cos_ref, sin_ref,
                    u_ref, qn_ref, qr_ref, ckv_ref, kr_ref, ckvb_ref, krb_ref):
    h = _rms(x_ref[...], g_ref[...]).astype(_CD)
    z = _dot(h, w_ref[...])
    u_ref[...] = z[:, _E_U:_E_CQ]
    cq = _rms(z[:, _E_CQ:_E_CKV], gq_ref[...]).astype(_CD)
    ckv = _rms(z[:, _E_CKV:_E_KR], gkv_ref[...])
    cos = cos_ref[...]
    sin = sin_ref[...]
    kr = z[:, _E_KR:_E_KR + MLA_ROPE] * cos[:, :MLA_ROPE] + z[:, _E_KRS:_E_KRS + MLA_ROPE] * sin[:, :MLA_ROPE]
    q = _dot(cq, wuq_ref[...])
    qn_ref[...] = q[:, _Q_NOPE:_Q_ROPE].astype(_CD)
    qr_ref[...] = (q[:, _Q_ROPE:_Q_ROPES] * cos + q[:, _Q_ROPES:_Q_END] * sin).astype(_CD)
    ckv_ref[...] = ckv
    kr_ref[...] = kr
    ckvb_ref[...] = ckv.astype(_CD)
    krb_ref[...] = kr.astype(_CD)


def _even_proj(x, g, w_all, gq, gkv, wuq_all, cos_t, sin_t):
    n, d = x.shape
    tm = _tile(n, TOKEN_TILE)
    nblk = cos_t.shape[0] // tm
    const = lambda i: (0, 0)
    row = lambda i: (i, 0)
    tab = lambda i: (i % nblk, 0)
    widths = (POOL_DIM, MLA_HEADS * MLA_NOPE, _QR, MLA_KV_LORA, MLA_ROPE, MLA_KV_LORA, MLA_ROPE)
    dtypes = (_F32, _CD, _CD, _F32, _F32, _CD, _CD)
    return pl.pallas_call(
        _even_proj_body, grid=(n // tm,),
        in_specs=[pl.BlockSpec((tm, d), row), pl.BlockSpec((1, d), const), pl.BlockSpec(w_all.shape, const),
                  pl.BlockSpec((1, MLA_Q_LORA), const), pl.BlockSpec((1, MLA_KV_LORA), const),
                  pl.BlockSpec(wuq_all.shape, const), pl.BlockSpec((tm, _QR), tab), pl.BlockSpec((tm, _QR), tab)],
        out_specs=[pl.BlockSpec((tm, w), row) for w in widths],
        out_shape=[jax.ShapeDtypeStruct((n, w), dt) for w, dt in zip(widths, dtypes)],
        compiler_params=_cparams(1), name="even_proj",
    )(x, g.reshape(1, d), w_all, gq.reshape(1, -1), gkv.reshape(1, -1), wuq_all, cos_t, sin_t)


_HIST_PAD = 2 * SUBLANES


def _pool_prompt_body(u_ref, w_ref, sc_ref, y_ref, ext_ref, *, T):
    ext_ref[0:_HIST_PAD, :] = jnp.zeros((_HIST_PAD, POOL_DIM), _F32)
    ext_ref[_HIST_PAD:_HIST_PAD + T, :] = u_ref[...]
    pos = lax.broadcasted_iota(jnp.int32, (T, 1), 0)
    for g, w in enumerate(POOL_WINDOWS):
        sl = slice(g * POOL_GROUP_DIM, (g + 1) * POOL_GROUP_DIM)
        s = ext_ref[_HIST_PAD:_HIST_PAD + T, sl]
        for j in range(1, w):
            s = s + ext_ref[_HIST_PAD - j:_HIST_PAD - j + T, sl]
        cnt = jnp.minimum(pos + 1, w).astype(_F32)
        pooled = (s / cnt - u_ref[:, sl]).astype(_CD)
        y_ref[:, sl] = (_dot(pooled, w_ref[g]) * sc_ref[:, sl]).astype(y_ref.dtype)


def _pool_prompt(u, w_pool, scale, B, T):
    return pl.pallas_call(
        functools.partial(_pool_prompt_body, T=T), grid=(B,),
        in_specs=[pl.BlockSpec((T, POOL_DIM), lambda b: (b, 0)),
                  pl.BlockSpec(w_pool.shape, lambda b: (0, 0, 0)), pl.BlockSpec((1, POOL_DIM), lambda b: (0, 0))],
        out_specs=pl.BlockSpec((T, POOL_DIM), lambda b: (b, 0)),
        out_shape=jax.ShapeDtypeStruct((B * T, POOL_DIM), _CD),
        scratch_shapes=[pltpu.VMEM((_HIST_PAD + T, POOL_DIM), _F32)],
        compiler_params=_cparams(1), name="pool_prompt",
    )(u, w_pool, scale.reshape(1, POOL_DIM))


def _pool_sample_body(ext_ref, w_ref, sc_ref, y_ref, *, Td, pos0):
    for t in range(Td):
        for g, w in enumerate(POOL_WINDOWS):
            sl = slice(g * POOL_GROUP_DIM, (g + 1) * POOL_GROUP_DIM)
            s = ext_ref[POOL_STATE + t, :, sl]
            for j in range(1, w):
                s = s + ext_ref[POOL_STATE + t - j, :, sl]
            cnt = float(min(pos0 + t + 1, w))
            pooled = (s / cnt - ext_ref[POOL_STATE + t, :, sl]).astype(_CD)
            y_ref[t, :, sl] = (_dot(pooled, w_ref[g]) * sc_ref[:, sl]).astype(y_ref.dtype)


def _pool_sample(ext, w_pool, scale, Td, pos0):
    _, Bd, _ = ext.shape
    return pl.pallas_call(
        functools.partial(_pool_sample_body, Td=Td, pos0=pos0),
        out_shape=jax.ShapeDtypeStruct((Td, Bd, POOL_DIM), _CD),
        compiler_params=pltpu.CompilerParams(vmem_limit_bytes=VMEM_LIMIT), name="pool_sample",
    )(ext, w_pool, scale.reshape(1, POOL_DIM))


def _lanes(x, n):
    return x[:, :n] if n <= LANES else jnp.concatenate([x] * (n // LANES), axis=1)


def _fold_lanes(p):
    out = p[:, :LANES]
    for c in range(1, p.shape[1] // LANES):
        out = out + p[:, c * LANES:(c + 1) * LANES]
    return out


def _softmax_step(s, pv, m_ref, l_ref, acc_ref):
    m_prev = m_ref[...]
    m_new = jnp.maximum(m_prev, jnp.max(s, axis=-1, keepdims=True))
    alpha = jnp.exp(m_prev - m_new)
    p = jnp.exp(s - _lanes(m_new, s.shape[1]))
    l_ref[...] = alpha * l_ref[...] + _fold_lanes(p)
    acc_ref[...] = _lanes(alpha, acc_ref.shape[-1]) * acc_ref[...] + pv(p.astype(_CD))
    m_ref[...] = m_new


def _softmax_rescale(s, m_ref, l_ref, acc_ref, p_ref):
    m_prev = m_ref[...]
    m_new = jnp.maximum(m_prev, jnp.max(s, axis=-1, keepdims=True))
    alpha = jnp.exp(m_prev - m_new)
    p = jnp.exp(s - _lanes(m_new, s.shape[1]))
    if l_ref is not None:
        l_ref[...] = alpha * l_ref[...] + _fold_lanes(p)
    acc_ref[...] = _lanes(alpha, acc_ref.shape[-1]) * acc_ref[...]
    p_ref[...] = p.astype(p_ref.dtype)
    m_ref[...] = m_new


def _mla_prompt_body(qn_ref, qr_ref, ckv_ref, kr_ref, wuk_ref, wuv_ref, o_ref, ql_s, qr_s, s_s, p_s, m_s, l_s, acc_s, *, TQ):
    i = pl.program_id(1)
    rows = [slice(h * TQ, (h + 1) * TQ) for h in range(MLA_HEADS)]
    for h in range(MLA_HEADS):
        ql_s[rows[h], :] = _dot(qn_ref[:, h * MLA_NOPE:(h + 1) * MLA_NOPE], wuk_ref[h]).astype(_CD)
        qr_s[rows[h], :] = qr_ref[:, h * MLA_ROPE:(h + 1) * MLA_ROPE]
    m_s[...] = jnp.full(m_s.shape, _NEG_INF, _F32)
    l_s[...] = jnp.zeros(l_s.shape, _F32)
    acc_s[...] = jnp.zeros(acc_s.shape, _F32)

    def block(off, visible):
        kc = ckv_ref[pl.ds(off, TQ), :]
        s_s[...] = (_dot_nt(ql_s[...], kc) + _dot_nt(qr_s[...], kr_ref[pl.ds(off, TQ), :])) * MLA_SCALE
        for r in rows:
            s = s_s[r, :]
            if visible is not None:
                s = jnp.where(visible, s, _NEG_INF)
            _softmax_rescale(s, m_s.at[r], l_s.at[r], acc_s.at[r], p_s.at[r])
        acc_s[...] = acc_s[...] + _dot(p_s[...], kc)

    def body(j, c):
        block(pl.multiple_of(j * TQ, TQ), None)
        return c

    lax.fori_loop(0, i, body, 0)
    tri = lax.broadcasted_iota(jnp.int32, (TQ, TQ), 1) <= lax.broadcasted_iota(jnp.int32, (TQ, TQ), 0)
    block(pl.multiple_of(i * TQ, TQ), tri)
    for h in range(MLA_HEADS):
        o_lat = (acc_s[rows[h], :] / jnp.sum(l_s[rows[h], :], axis=-1, keepdims=True)).astype(_CD)
        o_ref[:, h * MLA_V:(h + 1) * MLA_V] = _dot(o_lat, wuv_ref[h]).astype(o_ref.dtype)


def _mla_prompt(qn, qr, ckvb, krb, wuk, wuv, B, T):
    tq = _tile(T, ATTN_TILE)
    nq = T // tq
    R = MLA_HEADS * tq
    qmap = lambda b, i: (b * nq + i, 0)
    smap = lambda b, i: (b, 0)
    c3 = lambda b, i: (0, 0, 0)
    return pl.pallas_call(
        functools.partial(_mla_prompt_body, TQ=tq), grid=(B, nq),
        in_specs=[pl.BlockSpec((tq, qn.shape[1]), qmap), pl.BlockSpec((tq, qr.shape[1]), qmap),
                  pl.BlockSpec((T, MLA_KV_LORA), smap), pl.BlockSpec((T, MLA_ROPE), smap),
                  pl.BlockSpec(wuk.shape, c3), pl.BlockSpec(wuv.shape, c3)],
        out_specs=pl.BlockSpec((tq, MLA_HEADS * MLA_V), qmap),
        out_shape=jax.ShapeDtypeStruct((B * T, MLA_HEADS * MLA_V), _CD),
        scratch_shapes=[pltpu.VMEM((R, MLA_KV_LORA), _CD), pltpu.VMEM((R, MLA_ROPE), _CD),
                        pltpu.VMEM((R, tq), _F32), pltpu.VMEM((R, tq), _CD),
                        pltpu.VMEM((R, LANES), _F32), pltpu.VMEM((R, LANES), _F32), pltpu.VMEM((R, MLA_KV_LORA), _F32)],
        compiler_params=_cparams(2), name="mla_prompt",
    )(qn, qr, ckvb, krb, wuk, wuv)


def _mla_sample_body(pt_ref, qn_ref, qr_ref, nckv_ref, nkr_ref, wuk_ref, wuv_ref, *rest, PG, Td, TdP):
    ckv_pages, krt_pages = rest[:PG], rest[PG:2 * PG]
    o_ref, ql_s, qr_s, m_ref, l_ref, acc_ref = rest[2 * PG:]
    i = pl.program_id(1)
    R = MLA_HEADS * TdP

    @pl.when(i == 0)
    def _():
        for h in range(MLA_HEADS):
            ql_s[h * TdP:(h + 1) * TdP, :] = _dot(qn_ref[0][:, h * MLA_NOPE:(h + 1) * MLA_NOPE], wuk_ref[h])
            qr_s[h * TdP:(h + 1) * TdP, :] = qr_ref[0][:, h * MLA_ROPE:(h + 1) * MLA_ROPE].astype(_F32)
        m_ref[...] = jnp.full((R, LANES), _NEG_INF, _F32)
        l_ref[...] = jnp.zeros((R, LANES), _F32)
        acc_ref[...] = jnp.zeros((R, MLA_KV_LORA), _F32)

    ql = ql_s[...].astype(_CD)
    qr = qr_s[...].astype(_CD)
    kcs = [ckv_pages[j][...].astype(_CD) for j in range(PG)]
    s = jnp.concatenate([_dot_nt(ql, kcs[j]) + _dot(qr, krt_pages[j][...].astype(_CD)) for j in range(PG)], axis=1)

    def pv(p):
        out = _dot(p[:, :PAGE_SIZE], kcs[0])
        for j in range(1, PG):
            out = out + _dot(p[:, j * PAGE_SIZE:(j + 1) * PAGE_SIZE], kcs[j])
        return out

    _softmax_step(s * MLA_SCALE, pv, m_ref, l_ref, acc_ref)

    @pl.when(i == pl.num_programs(1) - 1)
    def _():
        kc = nckv_ref[0]
        s = (_dot_nt(ql, kc) + _dot_nt(qr, nkr_ref[0])) * MLA_SCALE
        t = lax.broadcasted_iota(jnp.int32, (R, PAGE_SIZE), 0) % TdP
        c = lax.broadcasted_iota(jnp.int32, (R, PAGE_SIZE), 1)
        s = jnp.where((c <= t) & (c < Td), s, _NEG_INF)
        _softmax_step(s, lambda p: _dot(p, kc), m_ref, l_ref, acc_ref)
        o_lat = (acc_ref[...] / jnp.sum(l_ref[...], axis=-1, keepdims=True)).astype(_CD)
        for h in range(MLA_HEADS):
            o_ref[0, :, h * MLA_V:(h + 1) * MLA_V] = _dot(o_lat[h * TdP:(h + 1) * TdP], wuv_ref[h])


def _page_spec(block, layer, PG, j):
    nd = len(block)

    def index_map(b, i, pt):
        return (layer, pt[b, i * PG + j]) + (0,) * (nd - 2)

    return pl.BlockSpec(block, index_map)


def _mla_sample(page_table, qn, qr, nckv, nkr, wuk, wuv, cache_ckv, cache_krt, layer, Td):
    Bd, TdP, _ = qn.shape
    n_pages = page_table.shape[1]
    pg = _tile(n_pages, PAGES_PER_STEP)
    seq = lambda b, i, pt: (b, 0, 0)
    c3 = lambda b, i, pt: (0, 0, 0)
    R = MLA_HEADS * TdP
    in_specs = [pl.BlockSpec((1, TdP, qn.shape[2]), seq), pl.BlockSpec((1, TdP, qr.shape[2]), seq),
                pl.BlockSpec((1, PAGE_SIZE, MLA_KV_LORA), seq), pl.BlockSpec((1, PAGE_SIZE, MLA_ROPE), seq),
                pl.BlockSpec(wuk.shape, c3), pl.BlockSpec(wuv.shape, c3)]
    in_specs += [_page_spec((None, None, PAGE_SIZE, MLA_KV_LORA), layer, pg, j) for j in range(pg)]
    in_specs += [_page_spec((None, None, MLA_ROPE, PAGE_SIZE), layer, pg, j) for j in range(pg)]
    return pl.pallas_call(
        functools.partial(_mla_sample_body, PG=pg, Td=Td, TdP=TdP),
        grid_spec=pltpu.PrefetchScalarGridSpec(
            num_scalar_prefetch=1, grid=(Bd, n_pages // pg), in_specs=in_specs,
            out_specs=pl.BlockSpec((1, TdP, MLA_HEADS * MLA_V), seq),
            scratch_shapes=[pltpu.VMEM((R, MLA_KV_LORA), _F32), pltpu.VMEM((R, MLA_ROPE), _F32),
                            pltpu.VMEM((R, LANES), _F32), pltpu.VMEM((R, LANES), _F32), pltpu.VMEM((R, MLA_KV_LORA), _F32)]),
        out_shape=jax.ShapeDtypeStruct((Bd, TdP, MLA_HEADS * MLA_V), _F32),
        compiler_params=_cparams(2), name="mla_sample",
    )(page_table, qn, qr, nckv, nkr, wuk, wuv, *([cache_ckv] * pg), *([cache_krt] * pg))


_O_Q, _O_K, _O_V, _O_QI, _O_KI, _O_WI, _O_END = 0, 1024, 1280, 1536, 2048, 2176, 2304
_KV = DSA_KV_HEADS * DSA_HEAD_DIM
_V1 = LANES


def _odd_proj_body(x_ref, g_ref, w_ref, q_ref, k_ref, v_ref, qi_ref, ki_ref, wi_ref, kb_ref, vb_ref, kib_ref, v1_ref):
    h = _rms(x_ref[...], g_ref[...]).astype(_CD)
    z = _dot(h, w_ref[...])
    q_ref[...] = z[:, _O_Q:_O_K].astype(_CD)
    k = z[:, _O_K:_O_V]
    v = z[:, _O_V:_O_QI]
    ki = z[:, _O_KI:_O_KI + IDX_DIM]
    k_ref[...] = k
    v_ref[...] = v
    qi_ref[...] = z[:, _O_QI:_O_KI].astype(_CD)
    ki_ref[...] = ki
    wi_ref[...] = z[:, _O_WI:_O_WI + IDX_HEADS] * (IDX_HEADS ** -0.5)
    kb_ref[...] = k.astype(_CD)
    vb_ref[...] = v.astype(_CD)
    kib_ref[...] = ki.astype(_CD)
    ones = jnp.ones((v.shape[0], _V1 - DSA_HEAD_DIM), _CD)
    for n in range(DSA_KV_HEADS):
        v1_ref[:, n * _V1:n * _V1 + DSA_HEAD_DIM] = v[:, n * DSA_HEAD_DIM:(n + 1) * DSA_HEAD_DIM].astype(_CD)
        v1_ref[:, n * _V1 + DSA_HEAD_DIM:(n + 1) * _V1] = ones


def _odd_proj(x, g, w_all):
    n, d = x.shape
    tm = _tile(n, TOKEN_TILE)
    const = lambda i: (0, 0)
    row = lambda i: (i, 0)
    widths = (DSA_HEADS * DSA_HEAD_DIM, _KV, _KV, IDX_HEADS * IDX_DIM, IDX_DIM, IDX_HEADS, _KV, _KV, IDX_DIM,
              DSA_KV_HEADS * _V1)
    dtypes = (_CD, _F32, _F32, _CD, _F32, _F32, _CD, _CD, _CD, _CD)
    return pl.pallas_call(
        _odd_proj_body, grid=(n // tm,),
        in_specs=[pl.BlockSpec((tm, d), row), pl.BlockSpec((1, d), const), pl.BlockSpec(w_all.shape, const)],
        out_specs=[pl.BlockSpec((tm, w), row) for w in widths],
        out_shape=[jax.ShapeDtypeStruct((n, w), dt) for w, dt in zip(widths, dtypes)],
        compiler_params=_cparams(1), name="odd_proj",
    )(x, g.reshape(1, d), w_all)


def _kth_largest_search(count, topk, idx_bits, shape, live=None):
    def value_bit(b, t_u):
        cand_u = t_u | lax.shift_left(jnp.int32(1), 31 - b)
        cand = cand_u ^ _MIN_I32
        return jnp.where(count(lambda k, idx: k >= cand) >= topk, cand_u, t_u)

    thr = lax.fori_loop(0, 32, value_bit, jnp.zeros(shape, jnp.int32)) ^ _MIN_I32
    need = topk - count(lambda k, idx: k > thr)
    n_ge = count(lambda k, idx: k >= thr)
    if live is not None:
        n_ge = jnp.where(live, n_ge, 0.0)

    def index_bit(b, cut):
        cand = cut | lax.shift_left(jnp.int32(1), idx_bits - 1 - b)
        return jnp.where(count(lambda k, idx: (k == thr) & (idx < cand)) < need, cand, cut)

    cut = lax.cond(jnp.max(n_ge) > topk,
                   lambda: lax.fori_loop(0, idx_bits, index_bit, jnp.zeros(shape, jnp.int32)),
                   lambda: jnp.full(shape, (1 << idx_bits) - 1, jnp.int32))
    return thr, cut


def _selected(kk, idx, thr, cut):
    return (kk > thr) | ((kk == thr) & (idx <= cut))


def _dsa_prompt_body(q_ref, qi_ref, wi_ref, k_ref, v1_ref, ki_ref, o_ref, key_s, bias_s, wib_s, qs, s_s, p_s, m_s, acc_s,
                     *, TQ, T, topk):
    i = pl.program_id(1)
    rowpos = i * TQ + lax.broadcasted_iota(jnp.int32, (TQ, 1), 0)
    col0 = lax.broadcasted_iota(jnp.int32, (1, TQ), 1)
    for h in range(IDX_HEADS):
        wib_s[h] = jnp.broadcast_to(wi_ref[:, h:h + 1], (TQ, LANES))

    def select(nch):
        blocks = [slice(j * TQ, (j + 1) * TQ) for j in range(nch)]
        for j, js in enumerate(blocks):
            kic = ki_ref[js, :]
            isc = jnp.zeros((TQ, TQ), _F32)
            for h in range(IDX_HEADS):
                s = _dot_nt(qi_ref[:, h * IDX_DIM:(h + 1) * IDX_DIM], kic) * IDX_SCALE
                isc = isc + jnp.maximum(s, 0.0) * _lanes(wib_s[h], TQ)
            if j == nch - 1:
                isc = jnp.where(j * TQ + col0 <= rowpos, isc, _NEG_INF)
            key_s[:, js] = _sort_key(isc)

        def count(pred):
            acc = None
            for j, js in enumerate(blocks):
                part = _fold_lanes(jnp.where(pred(key_s[:, js], j * TQ + col0), 1.0, 0.0))
                acc = part if acc is None else acc + part
            return jnp.sum(acc, axis=1, keepdims=True)

        thr, cut = _kth_largest_search(count, float(topk), max(1, (T - 1).bit_length()), (TQ, 1))
        for j, js in enumerate(blocks):
            sel = _selected(key_s[:, js], j * TQ + col0, thr, cut)
            if j == nch - 1:
                sel = sel & (j * TQ + col0 <= rowpos)
            bias_s[:, js] = jnp.where(sel, 0.0, _MASKED)

    for c in range(T // TQ):
        pl.when(i == c)(functools.partial(select, c + 1))

    m_s[...] = jnp.full(m_s.shape, _MASKED, _F32)
    acc_s[...] = jnp.zeros(acc_s.shape, _F32)
    G = DSA_GROUP * TQ
    for hh in range(DSA_HEADS):
        qs[hh * TQ:(hh + 1) * TQ, :] = q_ref[:, hh * DSA_HEAD_DIM:(hh + 1) * DSA_HEAD_DIM]

    def body(j, c):
        off = pl.multiple_of(j * TQ, TQ)
        bias = bias_s[:, pl.ds(off, TQ)]
        for n in range(DSA_KV_HEADS):
            g = slice(n * G, (n + 1) * G)
            s_s[g, :] = _dot_nt(qs[g, :], k_ref[pl.ds(off, TQ), n * DSA_HEAD_DIM:(n + 1) * DSA_HEAD_DIM]) * DSA_SCALE
        for hh in range(DSA_HEADS):
            r = slice(hh * TQ, (hh + 1) * TQ)
            _softmax_rescale(s_s[r, :] + bias, m_s.at[r], None, acc_s.at[r], p_s.at[r])
        for n in range(DSA_KV_HEADS):
            g = slice(n * G, (n + 1) * G)
            acc_s[g, :] = acc_s[g, :] + _dot(p_s[g, :], v1_ref[pl.ds(off, TQ), n * _V1:(n + 1) * _V1])
        return c

    lax.fori_loop(0, i + 1, body, 0)
    for hh in range(DSA_HEADS):
        a = acc_s[hh * TQ:(hh + 1) * TQ, :]
        o_ref[:, hh * DSA_HEAD_DIM:(hh + 1) * DSA_HEAD_DIM] = (a[:, :DSA_HEAD_DIM] / a[:, _V1 - DSA_HEAD_DIM:]).astype(o_ref.dtype)


def _dsa_prompt(q, qi, wi, kb, v1, kib, B, T):
    tq = _tile(T, ATTN_TILE)
    nq = T // tq
    topk = min(TOPK_MAX, T // 4)
    qmap = lambda b, i: (b * nq + i, 0)
    smap = lambda b, i: (b, 0)
    return pl.pallas_call(
        functools.partial(_dsa_prompt_body, TQ=tq, T=T, topk=topk), grid=(B, nq),
        in_specs=[pl.BlockSpec((tq, q.shape[1]), qmap), pl.BlockSpec((tq, qi.shape[1]), qmap),
                  pl.BlockSpec((tq, IDX_HEADS), qmap),
                  pl.BlockSpec((T, _KV), smap), pl.BlockSpec((T, v1.shape[1]), smap), pl.BlockSpec((T, IDX_DIM), smap)],
        out_specs=pl.BlockSpec((tq, DSA_HEADS * DSA_HEAD_DIM), qmap),
        out_shape=jax.ShapeDtypeStruct((B * T, DSA_HEADS * DSA_HEAD_DIM), _CD),
        scratch_shapes=[pltpu.VMEM((tq, T), jnp.int32), pltpu.VMEM((tq, T), _F32),
                        pltpu.VMEM((IDX_HEADS, tq, LANES), _F32), pltpu.VMEM((DSA_HEADS * tq, DSA_HEAD_DIM), _CD),
                        pltpu.VMEM((DSA_HEADS * tq, tq), _F32), pltpu.VMEM((DSA_HEADS * tq, tq), _CD),
                        pltpu.VMEM((DSA_HEADS * tq, LANES), _F32), pltpu.VMEM((DSA_HEADS * tq, _V1), _F32)],
        compiler_params=_cparams(2), name="dsa_prompt",
    )(q, qi, wi, kb, v1, kib)


def _dsa_select_body(pt_ref, qi_ref, wi_ref, nki_ref, *rest, PG, P, Td, TdP, topk):
    kit_pages = rest[:PG]
    key_ref, thr_ref, cut_ref, qs, ws = rest[PG:]
    i = pl.program_id(1)
    R = IDX_HEADS * TdP

    @pl.when(i == 0)
    def _():
        for h in range(IDX_HEADS):
            qs[h * TdP:(h + 1) * TdP, :] = qi_ref[0][:, h * IDX_DIM:(h + 1) * IDX_DIM].astype(_F32)
            ws[h * TdP:(h + 1) * TdP, :] = jnp.broadcast_to(wi_ref[0][:, h:h + 1], (TdP, PAGE_SIZE))

    def index_scores(qk):
        s = jnp.maximum(qk * IDX_SCALE, 0.0) * ws[...]
        isc = s[0:TdP]
        for h in range(1, IDX_HEADS):
            isc = isc + s[h * TdP:(h + 1) * TdP]
        return isc

    qb = qs[...].astype(_CD)
    for j in range(PG):
        off = pl.multiple_of((i * PG + j) * PAGE_SIZE, PAGE_SIZE)
        key_ref[0, :, pl.ds(off, PAGE_SIZE)] = _sort_key(index_scores(_dot(qb, kit_pages[j][...].astype(_CD))))

    @pl.when(i == pl.num_programs(1) - 1)
    def _():
        isc = index_scores(_dot_nt(qb, nki_ref[0]))
        t = lax.broadcasted_iota(jnp.int32, (TdP, PAGE_SIZE), 0)
        c = lax.broadcasted_iota(jnp.int32, (TdP, PAGE_SIZE), 1)
        key_ref[0, :, P:P + PAGE_SIZE] = _sort_key(jnp.where((c <= t) & (c < Td), isc, _NEG_INF))
        idx = lax.broadcasted_iota(jnp.int32, (1, P + PAGE_SIZE), 1)

        def count(pred):
            hit = jnp.where(pred(key_ref[0], idx), 1.0, 0.0)
            parts = [hit[:, c * LANES:(c + 1) * LANES] for c in range(hit.shape[1] // LANES)]
            while len(parts) > 1:
                parts = [a + b for a, b in zip(parts[::2], parts[1::2])] + ([parts[-1]] if len(parts) % 2 else [])
            return jnp.sum(parts[0], axis=1, keepdims=True)

        live = lax.broadcasted_iota(jnp.int32, (TdP, 1), 0) < Td
        thr, cut = _kth_largest_search(count, float(topk), (P + PAGE_SIZE - 1).bit_length(), (TdP, 1), live)
        thr_ref[0] = jnp.broadcast_to(thr, (TdP, LANES))
        cut_ref[0] = jnp.broadcast_to(cut, (TdP, LANES))


def _dsa_select(page_table, qi, wi, nki, cache_kit, layer, Td):
    Bd, TdP, _ = qi.shape
    n_pages = page_table.shape[1]
    P = n_pages * PAGE_SIZE
    pg = _tile(n_pages, PAGES_PER_STEP)
    topk = min(TOPK_MAX, (P + Td) // 4)
    seq = lambda b, i, pt: (b, 0, 0)
    R = IDX_HEADS * TdP
    in_specs = [pl.BlockSpec((1, TdP, qi.shape[2]), seq), pl.BlockSpec((1, TdP, IDX_HEADS), seq),
                pl.BlockSpec((1, PAGE_SIZE, IDX_DIM), seq)]
    in_specs += [_page_spec((None, None, IDX_DIM, PAGE_SIZE), layer, pg, j) for j in range(pg)]
    return pl.pallas_call(
        functools.partial(_dsa_select_body, PG=pg, P=P, Td=Td, TdP=TdP, topk=topk),
        grid_spec=pltpu.PrefetchScalarGridSpec(
            num_scalar_prefetch=1, grid=(Bd, n_pages // pg), in_specs=in_specs,
            out_specs=[pl.BlockSpec((1, TdP, P + PAGE_SIZE), seq), pl.BlockSpec((1, TdP, LANES), seq),
                       pl.BlockSpec((1, TdP, LANES), seq)],
            scratch_shapes=[pltpu.VMEM((R, IDX_DIM), _F32), pltpu.VMEM((R, PAGE_SIZE), _F32)]),
        out_shape=[jax.ShapeDtypeStruct((Bd, TdP, P + PAGE_SIZE), jnp.int32),
                   jax.ShapeDtypeStruct((Bd, TdP, LANES), jnp.int32), jax.ShapeDtypeStruct((Bd, TdP, LANES), jnp.int32)],
        compiler_params=_cparams(2), name="dsa_select",
    )(page_table, qi, wi, nki, *([cache_kit] * pg))


def _dsa_sample_body(pt_ref, q_ref, key_ref, nkey_ref, thr_ref, cut_ref, nk_ref, nv_ref, *rest, PG, P, Td, TdP):
    kt_pages, vt_pages = rest[:PG], rest[PG:2 * PG]
    o_ref, qs, m_ref, l_ref, acc_ref = rest[2 * PG:]
    i = pl.program_id(1)
    RH = DSA_GROUP * TdP
    R = DSA_KV_HEADS * RH
    heads = [slice(n * RH, (n + 1) * RH) for n in range(DSA_KV_HEADS)]

    @pl.when(i == 0)
    def _():
        for hh in range(DSA_HEADS):
            qs[hh * TdP:(hh + 1) * TdP, :] = q_ref[0][:, hh * DSA_HEAD_DIM:(hh + 1) * DSA_HEAD_DIM].astype(_F32)
        m_ref[...] = jnp.full((R, LANES), _MASKED, _F32)
        l_ref[...] = jnp.zeros((R, LANES), _F32)
        acc_ref[...] = jnp.zeros((R, DSA_HEAD_DIM), _F32)

    thr = thr_ref[0][:, 0:1]
    cut = cut_ref[0][:, 0:1]
    qb = qs[...].astype(_CD)

    def attend(kk, idx, visible, scores, pv):
        sel = _selected(kk, idx, thr, cut)
        if visible is not None:
            sel = sel & visible
        bias = jnp.concatenate([jnp.where(sel, 0.0, _MASKED)] * (R // TdP), axis=0)
        s = jnp.concatenate([scores(n, qb[heads[n]]) for n in range(DSA_KV_HEADS)], axis=0) * DSA_SCALE + bias
        _softmax_step(s, lambda p: jnp.concatenate([pv(n, p[heads[n]]) for n in range(DSA_KV_HEADS)], axis=0),
                      m_ref, l_ref, acc_ref)

    kts = [kt_pages[j][...].astype(_CD) for j in range(PG)]
    vts = [vt_pages[j][...].astype(_CD) for j in range(PG)]
    W = PG * PAGE_SIZE
    idx = i * W + lax.broadcasted_iota(jnp.int32, (TdP, W), 1)

    def past_pv(n, p):
        out = _dot_nt(p[:, :PAGE_SIZE], vts[0][n])
        for j in range(1, PG):
            out = out + _dot_nt(p[:, j * PAGE_SIZE:(j + 1) * PAGE_SIZE], vts[j][n])
        return out

    attend(key_ref[0], idx, None,
           lambda n, q: jnp.concatenate([_dot(q, kts[j][n]) for j in range(PG)], axis=1), past_pv)

    @pl.when(i == pl.num_programs(1) - 1)
    def _():
        t = lax.broadcasted_iota(jnp.int32, (TdP, PAGE_SIZE), 0)
        c = lax.broadcasted_iota(jnp.int32, (TdP, PAGE_SIZE), 1)
        attend(nkey_ref[0], P + c, (c <= t) & (c < Td),
               lambda n, q: _dot_nt(q, nk_ref[0][:, n * DSA_HEAD_DIM:(n + 1) * DSA_HEAD_DIM]),
               lambda n, p: _dot(p, nv_ref[0][:, n * DSA_HEAD_DIM:(n + 1) * DSA_HEAD_DIM]))
        o = acc_ref[...] / jnp.sum(l_ref[...], axis=-1, keepdims=True)
        for hh in range(DSA_HEADS):
            o_ref[0, :, hh * DSA_HEAD_DIM:(hh + 1) * DSA_HEAD_DIM] = o[hh * TdP:(hh + 1) * TdP]


def _dsa_sample(page_table, q, keys, nkeys, thr, cut, nk, nv, cache_kt, cache_vt, layer, Td):
    Bd, TdP, _ = q.shape
    n_pages = page_table.shape[1]
    P = n_pages * PAGE_SIZE
    pg = _tile(n_pages, PAGES_PER_STEP)
    seq = lambda b, i, pt: (b, 0, 0)
    R = DSA_HEADS * TdP
    in_specs = [pl.BlockSpec((1, TdP, q.shape[2]), seq),
                pl.BlockSpec((1, TdP, pg * PAGE_SIZE), lambda b, i, pt: (b, 0, i)),
                pl.BlockSpec((1, TdP, PAGE_SIZE), seq), pl.BlockSpec((1, TdP, LANES), seq),
                pl.BlockSpec((1, TdP, LANES), seq),
                pl.BlockSpec((1, PAGE_SIZE, _KV), seq), pl.BlockSpec((1, PAGE_SIZE, _KV), seq)]
    page_block = (None, None, DSA_KV_HEADS, DSA_HEAD_DIM, PAGE_SIZE)
    in_specs += [_page_spec(page_block, layer, pg, j) for j in range(pg)]
    in_specs += [_page_spec(page_block, layer, pg, j) for j in range(pg)]
    return pl.pallas_call(
        functools.partial(_dsa_sample_body, PG=pg, P=P, Td=Td, TdP=TdP),
        grid_spec=pltpu.PrefetchScalarGridSpec(
            num_scalar_prefetch=1, grid=(Bd, n_pages // pg), in_specs=in_specs,
            out_specs=pl.BlockSpec((1, TdP, DSA_HEADS * DSA_HEAD_DIM), seq),
            scratch_shapes=[pltpu.VMEM((R, DSA_HEAD_DIM), _F32), pltpu.VMEM((R, LANES), _F32),
                            pltpu.VMEM((R, LANES), _F32), pltpu.VMEM((R, DSA_HEAD_DIM), _F32)]),
        out_shape=jax.ShapeDtypeStruct((Bd, TdP, DSA_HEADS * DSA_HEAD_DIM), _F32),
        compiler_params=_cparams(2), name="dsa_sample",
    )(page_table, q, keys, nkeys, thr, cut, nk, nv, *([cache_kt] * pg), *([cache_vt] * pg))


def _rope_tables(pos):
    half = MLA_ROPE // 2
    inv = ROPE_THETA ** (-jnp.arange(half, dtype=_F32) / half)
    ang = pos.astype(_F32)[:, None] * inv[None, :]
    cos, sin = jnp.cos(ang), jnp.sin(ang)
    return (jnp.tile(jnp.concatenate([cos, cos], axis=1), (1, MLA_HEADS)),
            jnp.tile(jnp.concatenate([-sin, sin], axis=1), (1, MLA_HEADS)))


def _pack_cols(w, pieces, total):
    out = jnp.zeros((w.shape[0], total), w.dtype)
    for start, cols in pieces:
        out = out.at[:, start:start + len(cols)].set(w[:, np.asarray(cols)])
    return out.astype(_CD)


def _swap_halves(n):
    return (np.arange(n) + n // 2) % n


def _pad_rows(a, rows):
    return jnp.pad(a, ((0, 0), (0, rows - a.shape[1]), (0, 0)))


def kernel(x_prompt, x_sample, cache_mla_ckv, cache_mla_kr, cache_dsa_k, cache_dsa_v, cache_dsa_kidx, state_pool, page_table, ffn_norm, ffn_w1, ffn_w3, ffn_w2, mix_norm, final_norm, even_w_in, mla_q_norm, mla_kv_norm, mla_w_uq, mla_w_uk, mla_w_uv, pool_w, pool_scale, even_w_out, odd_w_in, odd_w_out):
    B, T, D = x_prompt.shape
    Bd, Td, _ = x_sample.shape
    depth = ffn_norm.shape[0]
    P = page_table.shape[1] * PAGE_SIZE
    TdP = -(-Td // SUBLANES) * SUBLANES
    assert D == D_MODEL and Td <= PAGE_SIZE and T >= POOL_STATE
    Np, Ns = B * T, Bd * Td
    tm_s = _tile(Ns, TOKEN_TILE)
    assert tm_s % Td == 0

    pos_p = jnp.arange(T, dtype=jnp.int32)
    pos_s = P + jnp.arange(Td, dtype=jnp.int32)
    cos_p, sin_p = _rope_tables(pos_p)
    cos_s, sin_s = (jnp.tile(a, (tm_s // Td, 1)) for a in _rope_tables(pos_s))

    def seq_rows(a, rows=TdP):
        return _pad_rows(a.reshape(Bd, Td, a.shape[1]), rows)

    cache_krt = jnp.swapaxes(cache_mla_kr, 2, 3)
    cache_kit = jnp.swapaxes(cache_dsa_kidx, 2, 3)
    cache_kt = jnp.transpose(cache_dsa_k, (0, 1, 3, 4, 2))
    cache_vt = jnp.transpose(cache_dsa_v, (0, 1, 3, 4, 2))

    xp = x_prompt.reshape(Np, D)
    xs = x_sample.reshape(Ns, D)
    pre_p, pre_s = (), ()
    outs = {k: [] for k in ("p_ckv", "p_kr", "p_k", "p_v", "p_ki", "p_pool", "s_ckv", "s_kr", "s_k", "s_v", "s_ki", "s_pool")}
    for l in range(depth):
        j = l // 2
        wa = [a.astype(_CD) for a in (ffn_w1[l, 0], ffn_w3[l, 0], ffn_w2[l, 0])]
        wb = [a.astype(_CD) for a in (ffn_w1[l, 1], ffn_w3[l, 1], ffn_w2[l, 1])]
        xp = _ffn(xp, ffn_norm[l, 0], *wa, pre=pre_p)
        xs = _ffn(xs, ffn_norm[l, 0], *wa, pre=pre_s)
        if l % 2 == 0:
            kr0 = POOL_DIM + MLA_Q_LORA + MLA_KV_LORA
            w_all = _pack_cols(even_w_in[j], [(_E_U, np.arange(kr0)), (_E_KR, kr0 + np.arange(MLA_ROPE)),
                                              (_E_KRS, kr0 + _swap_halves(MLA_ROPE))], _E_END)
            per_head = MLA_NOPE + MLA_ROPE
            heads = np.arange(MLA_HEADS)[:, None] * per_head
            wuq_all = _pack_cols(mla_w_uq[j], [(_Q_NOPE, (heads + np.arange(MLA_NOPE)[None]).ravel()),
                                               (_Q_ROPE, (heads + MLA_NOPE + np.arange(MLA_ROPE)[None]).ravel()),
                                               (_Q_ROPES, (heads + MLA_NOPE + _swap_halves(MLA_ROPE)[None]).ravel())], _Q_END)
            wuk = jnp.transpose(mla_w_uk[j], (1, 2, 0)).astype(_CD)
            wuv = jnp.transpose(mla_w_uv[j], (1, 0, 2)).astype(_CD)
            w_out = even_w_out[j].astype(_CD)
            pw = pool_w[j].astype(_CD)
            u, qn, qr, ckv, kr, ckvb, krb = _even_proj(xp, mix_norm[l], w_all, mla_q_norm[j], mla_kv_norm[j], wuq_all, cos_p, sin_p)
            y_pool = _pool_prompt(u, pw, pool_scale[j], B, T)
            o = _mla_prompt(qn, qr, ckvb, krb, wuk, wuv, B, T)
            pre_p = ((y_pool, w_out[:POOL_DIM]), (o, w_out[POOL_DIM:]))
            outs["p_ckv"].append(ckv.reshape(B, T, MLA_KV_LORA))
            outs["p_kr"].append(kr.reshape(B, T, MLA_ROPE))
            outs["p_pool"].append(u.reshape(B, T, POOL_DIM)[:, T - POOL_STATE:])
            u, qn, qr, ckv, kr, ckvb, krb = _even_proj(xs, mix_norm[l], w_all, mla_q_norm[j], mla_kv_norm[j], wuq_all, cos_s, sin_s)
            ext = jnp.concatenate([state_pool[j], u.reshape(Bd, Td, POOL_DIM)], axis=1)
            y_pool = _pool_sample(jnp.swapaxes(ext, 0, 1), pw, pool_scale[j], Td, P)
            y_pool = jnp.swapaxes(y_pool, 0, 1).reshape(Ns, POOL_DIM)
            o = _mla_sample(page_table, seq_rows(qn), seq_rows(qr), seq_rows(ckvb, PAGE_SIZE), seq_rows(krb, PAGE_SIZE),
                            wuk, wuv, cache_mla_ckv, cache_krt, j, Td)
            o = o[:, :Td].reshape(Ns, MLA_HEADS * MLA_V).astype(_CD)
            pre_s = ((y_pool, w_out[:POOL_DIM]), (o, w_out[POOL_DIM:]))
            outs["s_ckv"].append(ckv.reshape(Bd, Td, MLA_KV_LORA))
            outs["s_kr"].append(kr.reshape(Bd, Td, MLA_ROPE))
            outs["s_pool"].append(ext[:, -POOL_STATE:])
        else:
            o5 = (DSA_HEADS + 2 * DSA_KV_HEADS) * DSA_HEAD_DIM + IDX_HEADS * IDX_DIM + IDX_DIM
            o4 = o5 - IDX_DIM
            w_all = _pack_cols(odd_w_in[j], [(_O_Q, np.arange(o4)), (_O_KI, o4 + np.arange(IDX_DIM)),
                                             (_O_WI, o5 + np.arange(IDX_HEADS))], _O_END)
            w_out = odd_w_out[j].astype(_CD)
            q, k, v, qi, ki, wi, kb, vb, kib, v1 = _odd_proj(xp, mix_norm[l], w_all)
            o = _dsa_prompt(q, qi, wi, kb, v1, kib, B, T)
            pre_p = ((o, w_out),)
            outs["p_k"].append(k.reshape(B, T, DSA_KV_HEADS, DSA_HEAD_DIM))
            outs["p_v"].append(v.reshape(B, T, DSA_KV_HEADS, DSA_HEAD_DIM))
            outs["p_ki"].append(ki.reshape(B, T, IDX_DIM))
            q, k, v, qi, ki, wi, kb, vb, kib, _ = _odd_proj(xs, mix_norm[l], w_all)
            keys, thr, cut = _dsa_select(page_table, seq_rows(qi), seq_rows(wi), seq_rows(kib, PAGE_SIZE),
                                         cache_kit, j, Td)
            o = _dsa_sample(page_table, seq_rows(q), keys, keys[:, :, P:], thr, cut, seq_rows(kb, PAGE_SIZE),
                            seq_rows(vb, PAGE_SIZE), cache_kt, cache_vt, j, Td)
            o = o[:, :Td].reshape(Ns, DSA_HEADS * DSA_HEAD_DIM).astype(_CD)
            pre_s = ((o, w_out),)
            outs["s_k"].append(k.reshape(Bd, Td, DSA_KV_HEADS, DSA_HEAD_DIM))
            outs["s_v"].append(v.reshape(Bd, Td, DSA_KV_HEADS, DSA_HEAD_DIM))
            outs["s_ki"].append(ki.reshape(Bd, Td, IDX_DIM))
        last = l == depth - 1
        xp = _ffn(xp, ffn_norm[l, 1], *wb, pre=pre_p, final_g=final_norm if last else None)
        xs = _ffn(xs, ffn_norm[l, 1], *wb, pre=pre_s, final_g=final_norm if last else None)
        pre_p, pre_s = (), ()
    st = {k: jnp.stack(v) for k, v in outs.items()}
    return (xp.reshape(B, T, D), xs.reshape(Bd, Td, D),
            st["p_ckv"], st["p_kr"], st["p_k"], st["p_v"], st["p_ki"], st["p_pool"],
            st["s_ckv"], st["s_kr"], st["s_k"], st["s_v"], st["s_ki"], st["s_pool"])
```

```python
import functools
import math

import numpy as np
import jax
import jax.numpy as jnp
from jax import lax
from jax.experimental import pallas as pl
from jax.experimental.pallas import tpu as pltpu

D_MODEL = 1024
D_FF = 2 * D_MODEL
EPS = 1e-6
PAGE_SIZE = 128
POOL_WINDOWS = (2, 4, 8, 16)
POOL_GROUPS = len(POOL_WINDOWS)
POOL_GROUP_DIM = D_MODEL // 8
POOL_DIM = POOL_GROUPS * POOL_GROUP_DIM
POOL_STATE = max(POOL_WINDOWS) - 1
MLA_HEADS = 8
MLA_NOPE = D_MODEL // 16
MLA_ROPE = D_MODEL // 32
MLA_V = D_MODEL // 16
MLA_Q_LORA = 3 * D_MODEL // 8
MLA_KV_LORA = D_MODEL // 4
MLA_SCALE = (MLA_NOPE + MLA_ROPE) ** -0.5
ROPE_THETA = 10000.0
DSA_HEADS = 16
DSA_KV_HEADS = 4
DSA_GROUP = DSA_HEADS // DSA_KV_HEADS
DSA_HEAD_DIM = D_MODEL // 16
DSA_SCALE = DSA_HEAD_DIM ** -0.5
IDX_HEADS = 8
IDX_DIM = D_MODEL // 16
IDX_SCALE = IDX_DIM ** -0.5
TOPK_MAX = 256

LANES = 128
SUBLANES = 8
VMEM_LIMIT = 56 * 1024 * 1024

TOKEN_TILE = 512
FF_CHUNK = 512
ATTN_TILE = 256
SEARCH_ROWS = 64
PAGES_PER_STEP = 32

_CD = jnp.bfloat16
_F32 = jnp.float32
_NT = (((1,), (1,)), ((), ()))
_MIN_I32 = np.int32(-2 ** 31)
_MASKED = -1e30
_NEG_INF = float("-inf")


def _cparams(n_grid):
    return pltpu.CompilerParams(dimension_semantics=("arbitrary",) * n_grid, vmem_limit_bytes=VMEM_LIMIT)


def _dot(a, b):
    return jnp.dot(a, b, preferred_element_type=_F32)


def _dot_nt(a, b):
    return lax.dot_general(a, b, _NT, preferred_element_type=_F32)


def _rms(x, g):
    return x * lax.rsqrt(jnp.mean(x * x, axis=-1, keepdims=True) + EPS) * g


def _sort_key(x):
    bits = lax.bitcast_convert_type(x, jnp.int32)
    bits = jnp.where(bits == _MIN_I32, 0, bits)
    return jnp.where(bits < 0, bits ^ np.int32(0x7FFFFFFF), bits)


def _tile(n, cap):
    t = min(cap, n)
    assert n % t == 0, (n, t)
    return t


def _ffn_body(*refs, n_pre, final):
    x_ref, g_ref, w1_ref, w3_ref, w2_ref = refs[:5]
    pre = refs[5:5 + 2 * n_pre]
    rest = refs[5 + 2 * n_pre:]
    o_ref = rest[-1]
    x = x_ref[...]
    if n_pre:
        d = _dot(pre[0][...], pre[1][...])
        for k in range(1, n_pre):
            d = d + _dot(pre[2 * k][...], pre[2 * k + 1][...])
        x = x + d
    h = _rms(x, g_ref[...]).astype(_CD)
    acc = jnp.zeros_like(x)
    for c in range(D_FF // FF_CHUNK):
        sl = slice(c * FF_CHUNK, (c + 1) * FF_CHUNK)
        a = _dot(h, w1_ref[:, sl])
        b = _dot(h, w3_ref[:, sl])
        acc = acc + _dot((a * jax.nn.sigmoid(a) * b).astype(_CD), w2_ref[sl, :])
    y = x + 0.5 * acc
    if final:
        y = _rms(y, rest[0][...])
    o_ref[...] = y


def _ffn(x, g, w1, w3, w2, pre=(), final_g=None):
    n, d = x.shape
    tm = _tile(n, TOKEN_TILE)
    const = lambda i: (0, 0)
    row = lambda i: (i, 0)
    in_specs = [pl.BlockSpec((tm, d), row), pl.BlockSpec((1, d), const),
                pl.BlockSpec(w1.shape, const), pl.BlockSpec(w3.shape, const), pl.BlockSpec(w2.shape, const)]
    args = [x, g.reshape(1, d), w1, w3, w2]
    for a, w in pre:
        in_specs += [pl.BlockSpec((tm, a.shape[1]), row), pl.BlockSpec(w.shape, const)]
        args += [a, w]
    if final_g is not None:
        in_specs.append(pl.BlockSpec((1, d), const))
        args.append(final_g.reshape(1, d))
    return pl.pallas_call(
        functools.partial(_ffn_body, n_pre=len(pre), final=final_g is not None),
        grid=(n // tm,), in_specs=in_specs, out_specs=pl.BlockSpec((tm, d), row),
        out_shape=jax.ShapeDtypeStruct((n, d), _F32), compiler_params=_cparams(1), name="ffn_half",
    )(*args)


_E_U, _E_CQ, _E_CKV, _E_KR, _E_KRS, _E_END = 0, 512, 896, 1152, 1280, 1408
_Q_NOPE, _Q_ROPE, _Q_ROPES, _Q_END = 0, 512, 768, 1024
_QR = MLA_HEADS * MLA_ROPE


def _even_proj_body(x_ref, g_ref, w_ref, gq_ref, gkv_ref, wuq_ref, cos_ref, sin_ref,
                    u_ref, qn_ref, qr_ref, ckv_ref, kr_ref, ckvb_ref, krb_ref):
    h = _rms(x_ref[...], g_ref[...]).astype(_CD)
    z = _dot(h, w_ref[...])
    u_ref[...] = z[:, _E_U:_E_CQ]
    cq = _rms(z[:, _E_CQ:_E_CKV], gq_ref[...]).astype(_CD)
    ckv = _rms(z[:, _E_CKV:_E_KR], gkv_ref[...])
    cos = cos_ref[...]
    sin = sin_ref[...]
    kr = z[:, _E_KR:_E_KR + MLA_ROPE] * cos[:, :MLA_ROPE] + z[:, _E_KRS:_E_KRS + MLA_ROPE] * sin[:, :MLA_ROPE]
    q = _dot(cq, wuq_ref[...])
    qn_ref[...] = q[:, _Q_NOPE:_Q_ROPE].astype(_CD)
    qr_ref[...] = (q[:, _Q_ROPE:_Q_ROPES] * cos + q[:, _Q_ROPES:_Q_END] * sin).astype(_CD)
    ckv_ref[...] = ckv
    kr_ref[...] = kr
    ckvb_ref[...] = ckv.astype(_CD)
    krb_ref[...] = kr.astype(_CD)


def _even_proj(x, g, w_all, gq, gkv, wuq_all, cos_t, sin_t):
    n, d = x.shape
    tm = _tile(n, TOKEN_TILE)
    nblk = cos_t.shape[0] // tm
    const = lambda i: (0, 0)
    row = lambda i: (i, 0)
    tab = lambda i: (i % nblk, 0)
    widths = (POOL_DIM, MLA_HEADS * MLA_NOPE, _QR, MLA_KV_LORA, MLA_ROPE, MLA_KV_LORA, MLA_ROPE)
    dtypes = (_F32, _CD, _CD, _F32, _F32, _CD, _CD)
    return pl.pallas_call(
        _even_proj_body, grid=(n // tm,),
        in_specs=[pl.BlockSpec((tm, d), row), pl.BlockSpec((1, d), const), pl.BlockSpec(w_all.shape, const),
                  pl.BlockSpec((1, MLA_Q_LORA), const), pl.BlockSpec((1, MLA_KV_LORA), const),
                  pl.BlockSpec(wuq_all.shape, const), pl.BlockSpec((tm, _QR), tab), pl.BlockSpec((tm, _QR), tab)],
        out_specs=[pl.BlockSpec((tm, w), row) for w in widths],
        out_shape=[jax.ShapeDtypeStruct((n, w), dt) for w, dt in zip(widths, dtypes)],
        compiler_params=_cparams(1), name="even_proj",
    )(x, g.reshape(1, d), w_all, gq.reshape(1, -1), gkv.reshape(1, -1), wuq_all, cos_t, sin_t)


_HIST_PAD = 2 * SUBLANES


def _pool_prompt_body(u_ref, w_ref, sc_ref, y_ref, ext_ref, *, T):
    ext_ref[0:_HIST_PAD, :] = jnp.zeros((_HIST_PAD, POOL_DIM), _F32)
    ext_ref[_HIST_PAD:_HIST_PAD + T, :] = u_ref[...]
    pos = lax.broadcasted_iota(jnp.int32, (T, 1), 0)
    for g, w in enumerate(POOL_WINDOWS):
        sl = slice(g * POOL_GROUP_DIM, (g + 1) * POOL_GROUP_DIM)
        s = ext_ref[_HIST_PAD:_HIST_PAD + T, sl]
        for j in range(1, w):
            s = s + ext_ref[_HIST_PAD - j:_HIST_PAD - j + T, sl]
        cnt = jnp.minimum(pos + 1, w).astype(_F32)
        pooled = (s / cnt - u_ref[:, sl]).astype(_CD)
        y_ref[:, sl] = (_dot(pooled, w_ref[g]) * sc_ref[:, sl]).astype(y_ref.dtype)


def _pool_prompt(u, w_pool, scale, B, T):
    return pl.pallas_call(
        functools.partial(_pool_prompt_body, T=T), grid=(B,),
        in_specs=[pl.BlockSpec((T, POOL_DIM), lambda b: (b, 0)),
                  pl.BlockSpec(w_pool.shape, lambda b: (0, 0, 0)), pl.BlockSpec((1, POOL_DIM), lambda b: (0, 0))],
        out_specs=pl.BlockSpec((T, POOL_DIM), lambda b: (b, 0)),
        out_shape=jax.ShapeDtypeStruct((B * T, POOL_DIM), _CD),
        scratch_shapes=[pltpu.VMEM((_HIST_PAD + T, POOL_DIM), _F32)],
        compiler_params=_cparams(1), name="pool_prompt",
    )(u, w_pool, scale.reshape(1, POOL_DIM))


def _pool_sample_body(ext_ref, w_ref, sc_ref, y_ref, *, Td, pos0):
    for t in range(Td):
        for g, w in enumerate(POOL_WINDOWS):
            sl = slice(g * POOL_GROUP_DIM, (g + 1) * POOL_GROUP_DIM)
            s = ext_ref[POOL_STATE + t, :, sl]
            for j in range(1, w):
                s = s + ext_ref[POOL_STATE + t - j, :, sl]
            cnt = float(min(pos0 + t + 1, w))
            pooled = (s / cnt - ext_ref[POOL_STATE + t, :, sl]).astype(_CD)
            y_ref[t, :, sl] = (_dot(pooled, w_ref[g]) * sc_ref[:, sl]).astype(y_ref.dtype)


def _pool_sample(ext, w_pool, scale, Td, pos0):
    _, Bd, _ = ext.shape
    return pl.pallas_call(
        functools.partial(_pool_sample_body, Td=Td, pos0=pos0),
        out_shape=jax.ShapeDtypeStruct((Td, Bd, POOL_DIM), _CD),
        compiler_params=pltpu.CompilerParams(vmem_limit_bytes=VMEM_LIMIT), name="pool_sample",
    )(ext, w_pool, scale.reshape(1, POOL_DIM))


def _lanes(x, n):
    return x[:, :n] if n <= LANES else jnp.concatenate([x] * (n // LANES), axis=1)


def _fold_lanes(p):
    out = p[:, :LANES]
    for c in range(1, p.shape[1] // LANES):
        out = out + p[:, c * LANES:(c + 1) * LANES]
    return out


def _softmax_step(s, pv, m_ref, l_ref, acc_ref):
    m_prev = m_ref[...]
    m_new = jnp.maximum(m_prev, jnp.max(s, axis=-1, keepdims=True))
    alpha = jnp.exp(m_prev - m_new)
    p = jnp.exp(s - _lanes(m_new, s.shape[1]))
    l_ref[...] = alpha * l_ref[...] + _fold_lanes(p)
    acc_ref[...] = _lanes(alpha, acc_ref.shape[-1]) * acc_ref[...] + pv(p.astype(_CD))
    m_ref[...] = m_new


def _softmax_rescale(s, m_ref, l_ref, acc_ref, p_ref):
    m_prev = m_ref[...]
    m_new = jnp.maximum(m_prev, jnp.max(s, axis=-1, keepdims=True))
    alpha = jnp.exp(m_prev - m_new)
    p = jnp.exp(s - _lanes(m_new, s.shape[1]))
    if l_ref is not None:
        l_ref[...] = alpha * l_ref[...] + _fold_lanes(p)
    acc_ref[...] = _lanes(alpha, acc_ref.shape[-1]) * acc_ref[...]
    p_ref[...] = p.astype(p_ref.dtype)
    m_ref[...] = m_new


def _mla_prompt_body(qn_ref, qr_ref, ckv_ref, kr_ref, wuk_ref, wuv_ref, o_ref, ql_s, qr_s, s_s, p_s, m_s, l_s, acc_s, *, TQ):
    i = pl.program_id(1)
    rows = [slice(h * TQ, (h + 1) * TQ) for h in range(MLA_HEADS)]
    for h in range(MLA_HEADS):
        ql_s[rows[h], :] = _dot(qn_ref[:, h * MLA_NOPE:(h + 1) * MLA_NOPE], wuk_ref[h]).astype(_CD)
        qr_s[rows[h], :] = qr_ref[:, h * MLA_ROPE:(h + 1) * MLA_ROPE]
    m_s[...] = jnp.full(m_s.shape, _NEG_INF, _F32)
    l_s[...] = jnp.zeros(l_s.shape, _F32)
    acc_s[...] = jnp.zeros(acc_s.shape, _F32)

    def block(off, visible):
        kc = ckv_ref[pl.ds(off, TQ), :]
        s_s[...] = (_dot_nt(ql_s[...], kc) + _dot_nt(qr_s[...], kr_ref[pl.ds(off, TQ), :])) * MLA_SCALE
        for r in rows:
            s = s_s[r, :]
            if visible is not None:
                s = jnp.where(visible, s, _NEG_INF)
            _softmax_rescale(s, m_s.at[r], l_s.at[r], acc_s.at[r], p_s.at[r])
        acc_s[...] = acc_s[...] + _dot(p_s[...], kc)

    def body(j, c):
        block(pl.multiple_of(j * TQ, TQ), None)
        return c

    lax.fori_loop(0, i, body, 0)
    tri = lax.broadcasted_iota(jnp.int32, (TQ, TQ), 1) <= lax.broadcasted_iota(jnp.int32, (TQ, TQ), 0)
    block(pl.multiple_of(i * TQ, TQ), tri)
    for h in range(MLA_HEADS):
        o_lat = (acc_s[rows[h], :] / jnp.sum(l_s[rows[h], :], axis=-1, keepdims=True)).astype(_CD)
        o_ref[:, h * MLA_V:(h + 1) * MLA_V] = _dot(o_lat, wuv_ref[h]).astype(o_ref.dtype)


def _mla_prompt(qn, qr, ckvb, krb, wuk, wuv, B, T):
    tq = _tile(T, ATTN_TILE)
    nq = T // tq
    R = MLA_HEADS * tq
    qmap = lambda b, i: (b * nq + i, 0)
    smap = lambda b, i: (b, 0)
    c3 = lambda b, i: (0, 0, 0)
    return pl.pallas_call(
        functools.partial(_mla_prompt_body, TQ=tq), grid=(B, nq),
        in_specs=[pl.BlockSpec((tq, qn.shape[1]), qmap), pl.BlockSpec((tq, qr.shape[1]), qmap),
                  pl.BlockSpec((T, MLA_KV_LORA), smap), pl.BlockSpec((T, MLA_ROPE), smap),
                  pl.BlockSpec(wuk.shape, c3), pl.BlockSpec(wuv.shape, c3)],
        out_specs=pl.BlockSpec((tq, MLA_HEADS * MLA_V), qmap),
        out_shape=jax.ShapeDtypeStruct((B * T, MLA_HEADS * MLA_V), _CD),
        scratch_shapes=[pltpu.VMEM((R, MLA_KV_LORA), _CD), pltpu.VMEM((R, MLA_ROPE), _CD),
                        pltpu.VMEM((R, tq), _F32), pltpu.VMEM((R, tq), _CD),
                        pltpu.VMEM((R, LANES), _F32), pltpu.VMEM((R, LANES), _F32), pltpu.VMEM((R, MLA_KV_LORA), _F32)],
        compiler_params=_cparams(2), name="mla_prompt",
    )(qn, qr, ckvb, krb, wuk, wuv)


def _mla_sample_body(pt_ref, qn_ref, qr_ref, nckv_ref, nkr_ref, wuk_ref, wuv_ref, *rest, PG, Td, TdP):
    ckv_pages, krt_pages = rest[:PG], rest[PG:2 * PG]
    o_ref, ql_s, qr_s, m_ref, l_ref, acc_ref = rest[2 * PG:]
    i = pl.program_id(1)
    R = MLA_HEADS * TdP

    @pl.when(i == 0)
    def _():
        for h in range(MLA_HEADS):
            ql_s[h * TdP:(h + 1) * TdP, :] = _dot(qn_ref[0][:, h * MLA_NOPE:(h + 1) * MLA_NOPE], wuk_ref[h])
            qr_s[h * TdP:(h + 1) * TdP, :] = qr_ref[0][:, h * MLA_ROPE:(h + 1) * MLA_ROPE].astype(_F32)
        m_ref[...] = jnp.full((R, LANES), _NEG_INF, _F32)
        l_ref[...] = jnp.zeros((R, LANES), _F32)
        acc_ref[...] = jnp.zeros((R, MLA_KV_LORA), _F32)

    ql = ql_s[...].astype(_CD)
    qr = qr_s[...].astype(_CD)
    kcs = [ckv_pages[j][...].astype(_CD) for j in range(PG)]
    s = jnp.concatenate([_dot_nt(ql, kcs[j]) + _dot(qr, krt_pages[j][...].astype(_CD)) for j in range(PG)], axis=1)

    def pv(p):
        out = _dot(p[:, :PAGE_SIZE], kcs[0])
        for j in range(1, PG):
            out = out + _dot(p[:, j * PAGE_SIZE:(j + 1) * PAGE_SIZE], kcs[j])
        return out

    _softmax_step(s * MLA_SCALE, pv, m_ref, l_ref, acc_ref)

    @pl.when(i == pl.num_programs(1) - 1)
    def _():
        kc = nckv_ref[0]
        s = (_dot_nt(ql, kc) + _dot_nt(qr, nkr_ref[0])) * MLA_SCALE
        t = lax.broadcasted_iota(jnp.int32, (R, PAGE_SIZE), 0) % TdP
        c = lax.broadcasted_iota(jnp.int32, (R, PAGE_SIZE), 1)
        s = jnp.where((c <= t) & (c < Td), s, _NEG_INF)
        _softmax_step(s, lambda p: _dot(p, kc), m_ref, l_ref, acc_ref)
        o_lat = (acc_ref[...] / jnp.sum(l_ref[...], axis=-1, keepdims=True)).astype(_CD)
        for h in range(MLA_HEADS):
            o_ref[0, :, h * MLA_V:(h + 1) * MLA_V] = _dot(o_lat[h * TdP:(h + 1) * TdP], wuv_ref[h])


def _page_spec(block, layer, PG, j):
    nd = len(block)

    def index_map(b, i, pt):
        return (layer, pt[b, i * PG + j]) + (0,) * (nd - 2)

    return pl.BlockSpec(block, index_map)


def _mla_sample(page_table, qn, qr, nckv, nkr, wuk, wuv, cache_ckv, cache_krt, layer, Td):
    Bd, TdP, _ = qn.shape
    n_pages = page_table.shape[1]
    pg = _tile(n_pages, PAGES_PER_STEP)
    seq = lambda b, i, pt: (b, 0, 0)
    c3 = lambda b, i, pt: (0, 0, 0)
    R = MLA_HEADS * TdP
    in_specs = [pl.BlockSpec((1, TdP, qn.shape[2]), seq), pl.BlockSpec((1, TdP, qr.shape[2]), seq),
                pl.BlockSpec((1, PAGE_SIZE, MLA_KV_LORA), seq), pl.BlockSpec((1, PAGE_SIZE, MLA_ROPE), seq),
                pl.BlockSpec(wuk.shape, c3), pl.BlockSpec(wuv.shape, c3)]
    in_specs += [_page_spec((None, None, PAGE_SIZE, MLA_KV_LORA), layer, pg, j) for j in range(pg)]
    in_specs += [_page_spec((None, None, MLA_ROPE, PAGE_SIZE), layer, pg, j) for j in range(pg)]
    return pl.pallas_call(
        functools.partial(_mla_sample_body, PG=pg, Td=Td, TdP=TdP),
        grid_spec=pltpu.PrefetchScalarGridSpec(
            num_scalar_prefetch=1, grid=(Bd, n_pages // pg), in_specs=in_specs,
            out_specs=pl.BlockSpec((1, TdP, MLA_HEADS * MLA_V), seq),
            scratch_shapes=[pltpu.VMEM((R, MLA_KV_LORA), _F32), pltpu.VMEM((R, MLA_ROPE), _F32),
                            pltpu.VMEM((R, LANES), _F32), pltpu.VMEM((R, LANES), _F32), pltpu.VMEM((R, MLA_KV_LORA), _F32)]),
        out_shape=jax.ShapeDtypeStruct((Bd, TdP, MLA_HEADS * MLA_V), _F32),
        compiler_params=_cparams(2), name="mla_sample",
    )(page_table, qn, qr, nckv, nkr, wuk, wuv, *([cache_ckv] * pg), *([cache_krt] * pg))


_O_Q, _O_K, _O_V, _O_QI, _O_KI, _O_WI, _O_END = 0, 1024, 1280, 1536, 2048, 2176, 2304
_KV = DSA_KV_HEADS * DSA_HEAD_DIM
assert all(math.frexp(s)[0] == 0.5 for s in (DSA_SCALE, IDX_SCALE))
_V1 = LANES


def _odd_proj_body(x_ref, g_ref, w_ref, q_ref, k_ref, v_ref, qi_ref, ki_ref, wi_ref, kb_ref, vb_ref, kib_ref, v1_ref):
    h = _rms(x_ref[...], g_ref[...]).astype(_CD)
    z = _dot(h, w_ref[...])
    q_ref[...] = (z[:, _O_Q:_O_K] * DSA_SCALE).astype(_CD)
    k = z[:, _O_K:_O_V]
    v = z[:, _O_V:_O_QI]
    ki = z[:, _O_KI:_O_KI + IDX_DIM]
    k_ref[...] = k
    v_ref[...] = v
    qi_ref[...] = (z[:, _O_QI:_O_KI] * IDX_SCALE).astype(_CD)
    ki_ref[...] = ki
    wi_ref[...] = z[:, _O_WI:_O_WI + IDX_HEADS] * (IDX_HEADS ** -0.5)
    kb_ref[...] = k.astype(_CD)
    vb_ref[...] = v.astype(_CD)
    kib_ref[...] = ki.astype(_CD)
    ones = jnp.ones((v.shape[0], _V1 - DSA_HEAD_DIM), _CD)
    for n in range(DSA_KV_HEADS):
        v1_ref[:, n * _V1:n * _V1 + DSA_HEAD_DIM] = v[:, n * DSA_HEAD_DIM:(n + 1) * DSA_HEAD_DIM].astype(_CD)
        v1_ref[:, n * _V1 + DSA_HEAD_DIM:(n + 1) * _V1] = ones


def _odd_proj(x, g, w_all):
    n, d = x.shape
    tm = _tile(n, TOKEN_TILE)
    const = lambda i: (0, 0)
    row = lambda i: (i, 0)
    widths = (DSA_HEADS * DSA_HEAD_DIM, _KV, _KV, IDX_HEADS * IDX_DIM, IDX_DIM, IDX_HEADS, _KV, _KV, IDX_DIM,
              DSA_KV_HEADS * _V1)
    dtypes = (_CD, _F32, _F32, _CD, _F32, _F32, _CD, _CD, _CD, _CD)
    return pl.pallas_call(
        _odd_proj_body, grid=(n // tm,),
        in_specs=[pl.BlockSpec((tm, d), row), pl.BlockSpec((1, d), const), pl.BlockSpec(w_all.shape, const)],
        out_specs=[pl.BlockSpec((tm, w), row) for w in widths],
        out_shape=[jax.ShapeDtypeStruct((n, w), dt) for w, dt in zip(widths, dtypes)],
        compiler_params=_cparams(1), name="odd_proj",
    )(x, g.reshape(1, d), w_all)


def _kth_largest_search(count, topk, idx_bits, shape):
    def value_bit(b, t_u):
        cand_u = t_u | lax.shift_left(jnp.int32(1), 31 - b)
        cand = cand_u ^ _MIN_I32
        return jnp.where(count(lambda k, idx: k >= cand) >= topk, cand_u, t_u)

    thr = lax.fori_loop(0, 32, value_bit, jnp.zeros(shape, jnp.int32)) ^ _MIN_I32
    need = topk - count(lambda k, idx: k > thr)
    n_ge = count(lambda k, idx: k >= thr)

    def index_bit(b, cut):
        cand = cut | lax.shift_left(jnp.int32(1), idx_bits - 1 - b)
        return jnp.where(count(lambda k, idx: (k == thr) & (idx < cand)) < need, cand, cut)

    cut = lax.cond(jnp.max(n_ge) > topk,
                   lambda: lax.fori_loop(0, idx_bits, index_bit, jnp.zeros(shape, jnp.int32)),
                   lambda: jnp.full(shape, (1 << idx_bits) - 1, jnp.int32))
    return thr, cut


def _selected(kk, idx, thr, cut):
    return (kk > thr) | ((kk == thr) & (idx <= cut))


def _dsa_prompt_body(q_ref, qi_ref, wi_ref, k_ref, v1_ref, ki_ref, o_ref, key_s, bias_s, wib_s, qs, s_s, p_s, m_s, acc_s,
                     *, TQ, T, topk):
    i = pl.program_id(1)
    rowpos = i * TQ + lax.broadcasted_iota(jnp.int32, (TQ, 1), 0)
    col0 = lax.broadcasted_iota(jnp.int32, (1, TQ), 1)
    for h in range(IDX_HEADS):
        wib_s[h] = jnp.broadcast_to(wi_ref[:, h:h + 1], (TQ, LANES))

    def select(nch):
        blocks = [slice(j * TQ, (j + 1) * TQ) for j in range(nch)]
        for j, js in enumerate(blocks):
            kic = ki_ref[js, :]
            isc = jnp.zeros((TQ, TQ), _F32)
            for h in range(IDX_HEADS):
                s = _dot_nt(qi_ref[:, h * IDX_DIM:(h + 1) * IDX_DIM], kic)
                isc = isc + jnp.maximum(s, 0.0) * _lanes(wib_s[h], TQ)
            if j == nch - 1:
                isc = jnp.where(j * TQ + col0 <= rowpos, isc, _NEG_INF)
            key_s[:, js] = _sort_key(isc)

        def count(pred):
            acc = None
            for j, js in enumerate(blocks):
                part = _fold_lanes(jnp.where(pred(key_s[:, js], j * TQ + col0), 1.0, 0.0))
                acc = part if acc is None else acc + part
            return jnp.sum(acc, axis=1, keepdims=True)

        thr, cut = _kth_largest_search(count, float(topk), max(1, (T - 1).bit_length()), (TQ, 1))
        for j, js in enumerate(blocks):
            sel = _selected(key_s[:, js], j * TQ + col0, thr, cut)
            if j == nch - 1:
                sel = sel & (j * TQ + col0 <= rowpos)
            bias_s[:, js] = jnp.where(sel, 0.0, _MASKED)

    for c in range(T // TQ):
        pl.when(i == c)(functools.partial(select, c + 1))

    m_s[...] = jnp.full(m_s.shape, _MASKED, _F32)
    acc_s[...] = jnp.zeros(acc_s.shape, _F32)
    G = DSA_GROUP * TQ
    for hh in range(DSA_HEADS):
        qs[hh * TQ:(hh + 1) * TQ, :] = q_ref[:, hh * DSA_HEAD_DIM:(hh + 1) * DSA_HEAD_DIM]

    def body(j, c):
        off = pl.multiple_of(j * TQ, TQ)
        bias = bias_s[:, pl.ds(off, TQ)]
        for n in range(DSA_KV_HEADS):
            g = slice(n * G, (n + 1) * G)
            s_s[g, :] = _dot_nt(qs[g, :], k_ref[pl.ds(off, TQ), n * DSA_HEAD_DIM:(n + 1) * DSA_HEAD_DIM])
        for hh in range(DSA_HEADS):
            r = slice(hh * TQ, (hh + 1) * TQ)
            _softmax_rescale(s_s[r, :] + bias, m_s.at[r], None, acc_s.at[r], p_s.at[r])
        for n in range(DSA_KV_HEADS):
            g = slice(n * G, (n + 1) * G)
            acc_s[g, :] = acc_s[g, :] + _dot(p_s[g, :], v1_ref[pl.ds(off, TQ), n * _V1:(n + 1) * _V1])
        return c

    lax.fori_loop(0, i + 1, body, 0)
    for hh in range(DSA_HEADS):
        a = acc_s[hh * TQ:(hh + 1) * TQ, :]
        o_ref[:, hh * DSA_HEAD_DIM:(hh + 1) * DSA_HEAD_DIM] = (a[:, :DSA_HEAD_DIM] / a[:, _V1 - DSA_HEAD_DIM:]).astype(o_ref.dtype)


def _dsa_prompt(q, qi, wi, kb, v1, kib, B, T):
    tq = _tile(T, ATTN_TILE)
    nq = T // tq
    topk = min(TOPK_MAX, T // 4)
    qmap = lambda b, i: (b * nq + i, 0)
    smap = lambda b, i: (b, 0)
    return pl.pallas_call(
        functools.partial(_dsa_prompt_body, TQ=tq, T=T, topk=topk), grid=(B, nq),
        in_specs=[pl.BlockSpec((tq, q.shape[1]), qmap), pl.BlockSpec((tq, qi.shape[1]), qmap),
                  pl.BlockSpec((tq, IDX_HEADS), qmap),
                  pl.BlockSpec((T, _KV), smap), pl.BlockSpec((T, v1.shape[1]), smap), pl.BlockSpec((T, IDX_DIM), smap)],
        out_specs=pl.BlockSpec((tq, DSA_HEADS * DSA_HEAD_DIM), qmap),
        out_shape=jax.ShapeDtypeStruct((B * T, DSA_HEADS * DSA_HEAD_DIM), _CD),
        scratch_shapes=[pltpu.VMEM((tq, T), jnp.int32), pltpu.VMEM((tq, T), _F32),
                        pltpu.VMEM((IDX_HEADS, tq, LANES), _F32), pltpu.VMEM((DSA_HEADS * tq, DSA_HEAD_DIM), _CD),
                        pltpu.VMEM((DSA_HEADS * tq, tq), _F32), pltpu.VMEM((DSA_HEADS * tq, tq), _CD),
                        pltpu.VMEM((DSA_HEADS * tq, LANES), _F32), pltpu.VMEM((DSA_HEADS * tq, _V1), _F32)],
        compiler_params=_cparams(2), name="dsa_prompt",
    )(q, qi, wi, kb, v1, kib)


def _dsa_keys_body(pt_ref, qi_ref, wi_ref, nki_ref, *rest, PG, P, Td, TdP):
    kit_pages = rest[:PG]
    key_ref, qs, ws = rest[PG:]
    i = pl.program_id(1)

    @pl.when(i == 0)
    def _():
        for h in range(IDX_HEADS):
            qs[h * TdP:(h + 1) * TdP, :] = qi_ref[0][:, h * IDX_DIM:(h + 1) * IDX_DIM].astype(_F32)
            ws[h * TdP:(h + 1) * TdP, :] = jnp.broadcast_to(wi_ref[0][:, h:h + 1], (TdP, PAGE_SIZE))

    def index_scores(qk):
        s = jnp.maximum(qk, 0.0) * ws[...]
        isc = s[0:TdP]
        for h in range(1, IDX_HEADS):
            isc = isc + s[h * TdP:(h + 1) * TdP]
        return isc

    qb = qs[...].astype(_CD)
    for j in range(PG):
        off = pl.multiple_of((i * PG + j) * PAGE_SIZE, PAGE_SIZE)
        key_ref[0, :, pl.ds(off, PAGE_SIZE)] = _sort_key(index_scores(_dot(qb, kit_pages[j][...].astype(_CD))))

    @pl.when(i == pl.num_programs(1) - 1)
    def _():
        isc = index_scores(_dot_nt(qb, nki_ref[0]))
        t = lax.broadcasted_iota(jnp.int32, (TdP, PAGE_SIZE), 0)
        c = lax.broadcasted_iota(jnp.int32, (TdP, PAGE_SIZE), 1)
        key_ref[0, :, P:P + PAGE_SIZE] = _sort_key(jnp.where((c <= t) & (c < Td), isc, _NEG_INF))


def _dsa_keys(page_table, qi, wi, nki, cache_kit, layer, Td):
    Bd, TdP, _ = qi.shape
    n_pages = page_table.shape[1]
    P = n_pages * PAGE_SIZE
    pg = _tile(n_pages, PAGES_PER_STEP)
    seq = lambda b, i, pt: (b, 0, 0)
    R = IDX_HEADS * TdP
    in_specs = [pl.BlockSpec((1, TdP, qi.shape[2]), seq), pl.BlockSpec((1, TdP, IDX_HEADS), seq),
                pl.BlockSpec((1, PAGE_SIZE, IDX_DIM), seq)]
    in_specs += [_page_spec((None, None, IDX_DIM, PAGE_SIZE), layer, pg, j) for j in range(pg)]
    return pl.pallas_call(
        functools.partial(_dsa_keys_body, PG=pg, P=P, Td=Td, TdP=TdP),
        grid_spec=pltpu.PrefetchScalarGridSpec(
            num_scalar_prefetch=1, grid=(Bd, n_pages // pg), in_specs=in_specs,
            out_specs=pl.BlockSpec((1, TdP, P + PAGE_SIZE), seq),
            scratch_shapes=[pltpu.VMEM((R, IDX_DIM), _F32), pltpu.VMEM((R, PAGE_SIZE), _F32)]),
        out_shape=jax.ShapeDtypeStruct((Bd, TdP, P + PAGE_SIZE), jnp.int32),
        compiler_params=_cparams(2), name="dsa_keys",
    )(page_table, qi, wi, nki, *([cache_kit] * pg))


def _dsa_threshold_body(key_ref, thr_ref, cut_ref, *, topk):
    rows, width = key_ref.shape
    idx = lax.broadcasted_iota(jnp.int32, (1, width), 1)

    def count(pred):
        hit = jnp.where(pred(key_ref[...], idx), 1.0, 0.0)
        parts = [hit[:, c * LANES:(c + 1) * LANES] for c in range(width // LANES)]
        while len(parts) > 1:
            parts = [a + b for a, b in zip(parts[::2], parts[1::2])] + ([parts[-1]] if len(parts) % 2 else [])
        return jnp.sum(parts[0], axis=1, keepdims=True)

    thr, cut = _kth_largest_search(count, float(topk), (width - 1).bit_length(), (rows, 1))
    thr_ref[...] = jnp.broadcast_to(thr, (rows, LANES))
    cut_ref[...] = jnp.broadcast_to(cut, (rows, LANES))


def _dsa_threshold(keys, topk):
    rows, width = keys.shape
    rb = _tile(rows, SEARCH_ROWS)
    row = lambda i: (i, 0)
    return pl.pallas_call(
        functools.partial(_dsa_threshold_body, topk=topk), grid=(rows // rb,),
        in_specs=[pl.BlockSpec((rb, width), row)],
        out_specs=[pl.BlockSpec((rb, LANES), row), pl.BlockSpec((rb, LANES), row)],
        out_shape=[jax.ShapeDtypeStruct((rows, LANES), jnp.int32)] * 2,
        compiler_params=_cparams(1), name="dsa_threshold",
    )(keys)


def _dsa_sample_body(pt_ref, q_ref, key_ref, nkey_ref, thr_ref, cut_ref, nk_ref, nv_ref, *rest, PG, P, Td, TdP):
    kt_pages, vt_pages = rest[:PG], rest[PG:2 * PG]
    o_ref, qs, m_ref, l_ref, acc_ref = rest[2 * PG:]
    i = pl.program_id(1)
    RH = DSA_GROUP * TdP
    R = DSA_KV_HEADS * RH
    heads = [slice(n * RH, (n + 1) * RH) for n in range(DSA_KV_HEADS)]

    @pl.when(i == 0)
    def _():
        for hh in range(DSA_HEADS):
            qs[hh * TdP:(hh + 1) * TdP, :] = q_ref[0][:, hh * DSA_HEAD_DIM:(hh + 1) * DSA_HEAD_DIM].astype(_F32)
        m_ref[...] = jnp.full((R, LANES), _MASKED, _F32)
        l_ref[...] = jnp.zeros((R, LANES), _F32)
        acc_ref[...] = jnp.zeros((R, DSA_HEAD_DIM), _F32)

    thr = thr_ref[0][:, 0:1]
    cut = cut_ref[0][:, 0:1]
    qb = qs[...].astype(_CD)

    def attend(kk, idx, visible, scores, pv):
        sel = _selected(kk, idx, thr, cut)
        if visible is not None:
            sel = sel & visible
        bias = jnp.concatenate([jnp.where(sel, 0.0, _MASKED)] * (R // TdP), axis=0)
        s = jnp.concatenate([scores(n, qb[heads[n]]) for n in range(DSA_KV_HEADS)], axis=0) + bias
        _softmax_step(s, lambda p: jnp.concatenate([pv(n, p[heads[n]]) for n in range(DSA_KV_HEADS)], axis=0),
                      m_ref, l_ref, acc_ref)

    kts = [kt_pages[j][...].astype(_CD) for j in range(PG)]
    vts = [vt_pages[j][...].astype(_CD) for j in range(PG)]
    W = PG * PAGE_SIZE
    idx = i * W + lax.broadcasted_iota(jnp.int32, (TdP, W), 1)

    def past_pv(n, p):
        out = _dot_nt(p[:, :PAGE_SIZE], vts[0][n])
        for j in range(1, PG):
            out = out + _dot_nt(p[:, j * PAGE_SIZE:(j + 1) * PAGE_SIZE], vts[j][n])
        return out

    attend(key_ref[0], idx, None,
           lambda n, q: jnp.concatenate([_dot(q, kts[j][n]) for j in range(PG)], axis=1), past_pv)

    @pl.when(i == pl.num_programs(1) - 1)
    def _():
        t = lax.broadcasted_iota(jnp.int32, (TdP, PAGE_SIZE), 0)
        c = lax.broadcasted_iota(jnp.int32, (TdP, PAGE_SIZE), 1)
        attend(nkey_ref[0], P + c, (c <= t) & (c < Td),
               lambda n, q: _dot_nt(q, nk_ref[0][:, n * DSA_HEAD_DIM:(n + 1) * DSA_HEAD_DIM]),
               lambda n, p: _dot(p, nv_ref[0][:, n * DSA_HEAD_DIM:(n + 1) * DSA_HEAD_DIM]))
        o = acc_ref[...] / jnp.sum(l_ref[...], axis=-1, keepdims=True)
        for hh in range(DSA_HEADS):
            o_ref[0, :, hh * DSA_HEAD_DIM:(hh + 1) * DSA_HEAD_DIM] = o[hh * TdP:(hh + 1) * TdP]


def _dsa_sample(page_table, q, keys, nkeys, thr, cut, nk, nv, cache_kt, cache_vt, layer, Td):
    Bd, TdP, _ = q.shape
    n_pages = page_table.shape[1]
    P = n_pages * PAGE_SIZE
    pg = _tile(n_pages, PAGES_PER_STEP)
    seq = lambda b, i, pt: (b, 0, 0)
    R = DSA_HEADS * TdP
    in_specs = [pl.BlockSpec((1, TdP, q.shape[2]), seq),
                pl.BlockSpec((1, TdP, pg * PAGE_SIZE), lambda b, i, pt: (b, 0, i)),
                pl.BlockSpec((1, TdP, PAGE_SIZE), seq), pl.BlockSpec((1, TdP, LANES), seq),
                pl.BlockSpec((1, TdP, LANES), seq),
                pl.BlockSpec((1, PAGE_SIZE, _KV), seq), pl.BlockSpec((1, PAGE_SIZE, _KV), seq)]
    page_block = (None, None, DSA_KV_HEADS, DSA_HEAD_DIM, PAGE_SIZE)
    in_specs += [_page_spec(page_block, layer, pg, j) for j in range(pg)]
    in_specs += [_page_spec(page_block, layer, pg, j) for j in range(pg)]
    return pl.pallas_call(
        functools.partial(_dsa_sample_body, PG=pg, P=P, Td=Td, TdP=TdP),
        grid_spec=pltpu.PrefetchScalarGridSpec(
            num_scalar_prefetch=1, grid=(Bd, n_pages // pg), in_specs=in_specs,
            out_specs=pl.BlockSpec((1, TdP, DSA_HEADS * DSA_HEAD_DIM), seq),
            scratch_shapes=[pltpu.VMEM((R, DSA_HEAD_DIM), _F32), pltpu.VMEM((R, LANES), _F32),
                            pltpu.VMEM((R, LANES), _F32), pltpu.VMEM((R, DSA_HEAD_DIM), _F32)]),
        out_shape=jax.ShapeDtypeStruct((Bd, TdP, DSA_HEADS * DSA_HEAD_DIM), _F32),
        compiler_params=_cparams(2), name="dsa_sample",
    )(page_table, q, keys, nkeys, thr, cut, nk, nv, *([cache_kt] * pg), *([cache_vt] * pg))


def _rope_tables(pos):
    half = MLA_ROPE // 2
    inv = ROPE_THETA ** (-jnp.arange(half, dtype=_F32) / half)
    ang = pos.astype(_F32)[:, None] * inv[None, :]
    cos, sin = jnp.cos(ang), jnp.sin(ang)
    return (jnp.tile(jnp.concatenate([cos, cos], axis=1), (1, MLA_HEADS)),
            jnp.tile(jnp.concatenate([-sin, sin], axis=1), (1, MLA_HEADS)))


def _pack_cols(w, pieces, total):
    out = jnp.zeros((w.shape[0], total), w.dtype)
    for start, cols in pieces:
        out = out.at[:, start:start + len(cols)].set(w[:, np.asarray(cols)])
    return out.astype(_CD)


def _swap_halves(n):
    return (np.arange(n) + n // 2) % n


def _pad_rows(a, rows):
    return jnp.pad(a, ((0, 0), (0, rows - a.shape[1]), (0, 0)))


def kernel(x_prompt, x_sample, cache_mla_ckv, cache_mla_kr, cache_dsa_k, cache_dsa_v, cache_dsa_kidx, state_pool, page_table, ffn_norm, ffn_w1, ffn_w3, ffn_w2, mix_norm, final_norm, even_w_in, mla_q_norm, mla_kv_norm, mla_w_uq, mla_w_uk, mla_w_uv, pool_w, pool_scale, even_w_out, odd_w_in, odd_w_out):
    B, T, D = x_prompt.shape
    Bd, Td, _ = x_sample.shape
    depth = ffn_norm.shape[0]
    P = page_table.shape[1] * PAGE_SIZE
    TdP = -(-Td // SUBLANES) * SUBLANES
    assert D == D_MODEL and Td <= PAGE_SIZE and T >= POOL_STATE
    Np, Ns = B * T, Bd * Td
    tm_s = _tile(Ns, TOKEN_TILE)
    assert tm_s % Td == 0

    pos_p = jnp.arange(T, dtype=jnp.int32)
    pos_s = P + jnp.arange(Td, dtype=jnp.int32)
    cos_p, sin_p = _rope_tables(pos_p)
    cos_s, sin_s = (jnp.tile(a, (tm_s // Td, 1)) for a in _rope_tables(pos_s))

    def seq_rows(a, rows=TdP):
        return _pad_rows(a.reshape(Bd, Td, a.shape[1]), rows)

    cache_krt = jnp.swapaxes(cache_mla_kr, 2, 3)
    cache_kit = jnp.swapaxes(cache_dsa_kidx, 2, 3)
    cache_kt = jnp.transpose(cache_dsa_k, (0, 1, 3, 4, 2))
    cache_vt = jnp.transpose(cache_dsa_v, (0, 1, 3, 4, 2))

    xp = x_prompt.reshape(Np, D)
    xs = x_sample.reshape(Ns, D)
    pre_p, pre_s = (), ()
    outs = {k: [] for k in ("p_ckv", "p_kr", "p_k", "p_v", "p_ki", "p_pool", "s_ckv", "s_kr", "s_k", "s_v", "s_ki", "s_pool")}
    for l in range(depth):
        j = l // 2
        wa = [a.astype(_CD) for a in (ffn_w1[l, 0], ffn_w3[l, 0], ffn_w2[l, 0])]
        wb = [a.astype(_CD) for a in (ffn_w1[l, 1], ffn_w3[l, 1], ffn_w2[l, 1])]
        xp = _ffn(xp, ffn_norm[l, 0], *wa, pre=pre_p)
        xs = _ffn(xs, ffn_norm[l, 0], *wa, pre=pre_s)
        if l % 2 == 0:
            kr0 = POOL_DIM + MLA_Q_LORA + MLA_KV_LORA
            w_all = _pack_cols(even_w_in[j], [(_E_U, np.arange(kr0)), (_E_KR, kr0 + np.arange(MLA_ROPE)),
                                              (_E_KRS, kr0 + _swap_halves(MLA_ROPE))], _E_END)
            per_head = MLA_NOPE + MLA_ROPE
            heads = np.arange(MLA_HEADS)[:, None] * per_head
            wuq_all = _pack_cols(mla_w_uq[j], [(_Q_NOPE, (heads + np.arange(MLA_NOPE)[None]).ravel()),
                                               (_Q_ROPE, (heads + MLA_NOPE + np.arange(MLA_ROPE)[None]).ravel()),
                                               (_Q_ROPES, (heads + MLA_NOPE + _swap_halves(MLA_ROPE)[None]).ravel())], _Q_END)
            wuk = jnp.transpose(mla_w_uk[j], (1, 2, 0)).astype(_CD)
            wuv = jnp.transpose(mla_w_uv[j], (1, 0, 2)).astype(_CD)
            w_out = even_w_out[j].astype(_CD)
            pw = pool_w[j].astype(_CD)
            u, qn, qr, ckv, kr, ckvb, krb = _even_proj(xp, mix_norm[l], w_all, mla_q_norm[j], mla_kv_norm[j], wuq_all, cos_p, sin_p)
            y_pool = _pool_prompt(u, pw, pool_scale[j], B, T)
            o = _mla_prompt(qn, qr, ckvb, krb, wuk, wuv, B, T)
            pre_p = ((y_pool, w_out[:POOL_DIM]), (o, w_out[POOL_DIM:]))
            outs["p_ckv"].append(ckv.reshape(B, T, MLA_KV_LORA))
            outs["p_kr"].append(kr.reshape(B, T, MLA_ROPE))
            outs["p_pool"].append(u.reshape(B, T, POOL_DIM)[:, T - POOL_STATE:])
            u, qn, qr, ckv, kr, ckvb, krb = _even_proj(xs, mix_norm[l], w_all, mla_q_norm[j], mla_kv_norm[j], wuq_all, cos_s, sin_s)
            ext = jnp.concatenate([state_pool[j], u.reshape(Bd, Td, POOL_DIM)], axis=1)
            y_pool = _pool_sample(jnp.swapaxes(ext, 0, 1), pw, pool_scale[j], Td, P)
            y_pool = jnp.swapaxes(y_pool, 0, 1).reshape(Ns, POOL_DIM)
            o = _mla_sample(page_table, seq_rows(qn), seq_rows(qr), seq_rows(ckvb, PAGE_SIZE), seq_rows(krb, PAGE_SIZE),
                            wuk, wuv, cache_mla_ckv, cache_krt, j, Td)
            o = o[:, :Td].reshape(Ns, MLA_HEADS * MLA_V).astype(_CD)
            pre_s = ((y_pool, w_out[:POOL_DIM]), (o, w_out[POOL_DIM:]))
            outs["s_ckv"].append(ckv.reshape(Bd, Td, MLA_KV_LORA))
            outs["s_kr"].append(kr.reshape(Bd, Td, MLA_ROPE))
            outs["s_pool"].append(ext[:, -POOL_STATE:])
        else:
            o5 = (DSA_HEADS + 2 * DSA_KV_HEADS) * DSA_HEAD_DIM + IDX_HEADS * IDX_DIM + IDX_DIM
            o4 = o5 - IDX_DIM
            w_all = _pack_cols(odd_w_in[j], [(_O_Q, np.arange(o4)), (_O_KI, o4 + np.arange(IDX_DIM)),
                                             (_O_WI, o5 + np.arange(IDX_HEADS))], _O_END)
            w_out = odd_w_out[j].astype(_CD)
            q, k, v, qi, ki, wi, kb, vb, kib, v1 = _odd_proj(xp, mix_norm[l], w_all)
            o = _dsa_prompt(q, qi, wi, kb, v1, kib, B, T)
            pre_p = ((o, w_out),)
            outs["p_k"].append(k.reshape(B, T, DSA_KV_HEADS, DSA_HEAD_DIM))
            outs["p_v"].append(v.reshape(B, T, DSA_KV_HEADS, DSA_HEAD_DIM))
            outs["p_ki"].append(ki.reshape(B, T, IDX_DIM))
            q, k, v, qi, ki, wi, kb, vb, kib, _ = _odd_proj(xs, mix_norm[l], w_all)
            keys = _dsa_keys(page_table, seq_rows(qi), seq_rows(wi), seq_rows(kib, PAGE_SIZE), cache_kit, j, Td)
            thr, cut = _dsa_threshold(keys[:, :Td].reshape(Ns, P + PAGE_SIZE), min(TOPK_MAX, (P + Td) // 4))
            o = _dsa_sample(page_table, seq_rows(q), keys, keys[:, :, P:], seq_rows(thr), seq_rows(cut), seq_rows(kb, PAGE_SIZE),
                            seq_rows(vb, PAGE_SIZE), cache_kt, cache_vt, j, Td)
            o = o[:, :Td].reshape(Ns, DSA_HEADS * DSA_HEAD_DIM).astype(_CD)
            pre_s = ((o, w_out),)
            outs["s_k"].append(k.reshape(Bd, Td, DSA_KV_HEADS, DSA_HEAD_DIM))
            outs["s_v"].append(v.reshape(Bd, Td, DSA_KV_HEADS, DSA_HEAD_DIM))
            outs["s_ki"].append(ki.reshape(Bd, Td, IDX_DIM))
        last = l == depth - 1
        xp = _ffn(xp, ffn_norm[l, 1], *wb, pre=pre_p, final_g=final_norm if last else None)
        xs = _ffn(xs, ffn_norm[l, 1], *wb, pre=pre_s, final_g=final_norm if last else None)
        pre_p, pre_s = (), ()
    st = {k: jnp.stack(v) for k, v in outs.items()}
    return (xp.reshape(B, T, D), xs.reshape(Bd, Td, D),
            st["p_ckv"], st["p_kr"], st["p_k"], st["p_v"], st["p_ki"], st["p_pool"],
            st["s_ckv"], st["s_kr"], st["s_k"], st["s_v"], st["s_ki"], st["s_pool"])
```

```python
import functools
import math

import numpy as np
import jax
import jax.numpy as jnp
from jax import lax
from jax.experimental import pallas as pl
from jax.experimental.pallas import tpu as pltpu

D_MODEL = 1024
D_FF = 2 * D_MODEL
EPS = 1e-6
PAGE_SIZE = 128
POOL_WINDOWS = (2, 4, 8, 16)
POOL_GROUPS = len(POOL_WINDOWS)
POOL_GROUP_DIM = D_MODEL // 8
POOL_DIM = POOL_GROUPS * POOL_GROUP_DIM
POOL_STATE = max(POOL_WINDOWS) - 1
MLA_HEADS = 8
MLA_NOPE = D_MODEL // 16
MLA_ROPE = D_MODEL // 32
MLA_V = D_MODEL // 16
MLA_Q_LORA = 3 * D_MODEL // 8
MLA_KV_LORA = D_MODEL // 4
MLA_SCALE = (MLA_NOPE + MLA_ROPE) ** -0.5
ROPE_THETA = 10000.0
DSA_HEADS = 16
DSA_KV_HEADS = 4
DSA_GROUP = DSA_HEADS // DSA_KV_HEADS
DSA_HEAD_DIM = D_MODEL // 16
DSA_SCALE = DSA_HEAD_DIM ** -0.5
IDX_HEADS = 8
IDX_DIM = D_MODEL // 16
IDX_SCALE = IDX_DIM ** -0.5
TOPK_MAX = 256

LANES = 128
SUBLANES = 8
VMEM_LIMIT = 56 * 1024 * 1024

TOKEN_TILE = 512
FF_CHUNK = 512
ATTN_TILE = 256
SEARCH_ROWS = 64
PAGES_PER_STEP = 32

_CD = jnp.bfloat16
_F32 = jnp.float32
_NT = (((1,), (1,)), ((), ()))
_MIN_I32 = np.int32(-2 ** 31)
_MASKED = -1e30
_NEG_INF = float("-inf")


def _cparams(n_grid):
    return pltpu.CompilerParams(dimension_semantics=("arbitrary",) * n_grid, vmem_limit_bytes=VMEM_LIMIT)


def _dot(a, b):
    return jnp.dot(a, b, preferred_element_type=_F32)


def _dot_nt(a, b):
    return lax.dot_general(a, b, _NT, preferred_element_type=_F32)


def _rms(x, g):
    return x * lax.rsqrt(jnp.mean(x * x, axis=-1, keepdims=True) + EPS) * g


def _sort_key(x):
    bits = lax.bitcast_convert_type(x, jnp.int32)
    return jnp.where(bits < 0, bits ^ np.int32(0x7FFFFFFF), bits)


def _tile(n, cap):
    t = min(cap, n)
    assert n % t == 0, (n, t)
    return t


def _ffn_body(*refs, n_pre, final):
    x_ref, g_ref, w1_ref, w3_ref, w2_ref = refs[:5]
    pre = refs[5:5 + 2 * n_pre]
    rest = refs[5 + 2 * n_pre:]
    o_ref = rest[-1]
    x = x_ref[...]
    if n_pre:
        d = _dot(pre[0][...], pre[1][...])
        for k in range(1, n_pre):
            d = d + _dot(pre[2 * k][...], pre[2 * k + 1][...])
        x = x + d
    h = _rms(x, g_ref[...]).astype(_CD)
    acc = jnp.zeros_like(x)
    for c in range(D_FF // FF_CHUNK):
        sl = slice(c * FF_CHUNK, (c + 1) * FF_CHUNK)
        a = _dot(h, w1_ref[:, sl])
        b = _dot(h, w3_ref[:, sl])
        acc = acc + _dot((a * jax.nn.sigmoid(a) * b).astype(_CD), w2_ref[sl, :])
    y = x + 0.5 * acc
    if final:
        y = _rms(y, rest[0][...])
    o_ref[...] = y


def _ffn(x, g, w1, w3, w2, pre=(), final_g=None):
    n, d = x.shape
    tm = _tile(n, TOKEN_TILE)
    const = lambda i: (0, 0)
    row = lambda i: (i, 0)
    in_specs = [pl.BlockSpec((tm, d), row), pl.BlockSpec((1, d), const),
                pl.BlockSpec(w1.shape, const), pl.BlockSpec(w3.shape, const), pl.BlockSpec(w2.shape, const)]
    args = [x, g.reshape(1, d), w1, w3, w2]
    for a, w in pre:
        in_specs += [pl.BlockSpec((tm, a.shape[1]), row), pl.BlockSpec(w.shape, const)]
        args += [a, w]
    if final_g is not None:
        in_specs.append(pl.BlockSpec((1, d), const))
        args.append(final_g.reshape(1, d))
    return pl.pallas_call(
        functools.partial(_ffn_body, n_pre=len(pre), final=final_g is not None),
        grid=(n // tm,), in_specs=in_specs, out_specs=pl.BlockSpec((tm, d), row),
        out_shape=jax.ShapeDtypeStruct((n, d), _F32), compiler_params=_cparams(1), name="ffn_half",
    )(*args)


_E_U, _E_CQ, _E_CKV, _E_KR, _E_KRS, _E_END = 0, 512, 896, 1152, 1280, 1408
_Q_NOPE, _Q_ROPE, _Q_ROPES, _Q_END = 0, 512, 768, 1024
_QR = MLA_HEADS * MLA_ROPE


def _even_proj_body(x_ref, g_ref, w_ref, gq_ref, gkv_ref, wuq_ref, cos_ref, sin_ref,
                    u_ref, qn_ref, qr_ref, ckv_ref, kr_ref, ckvb_ref, krb_ref):
    h = _rms(x_ref[...], g_ref[...]).astype(_CD)
    z = _dot(h, w_ref[...])
    u_ref[...] = z[:, _E_U:_E_CQ]
    cq = _rms(z[:, _E_CQ:_E_CKV], gq_ref[...]).astype(_CD)
    ckv = _rms(z[:, _E_CKV:_E_KR], gkv_ref[...])
    cos = cos_ref[...]
    sin = sin_ref[...]
    kr = z[:, _E_KR:_E_KR + MLA_ROPE] * cos[:, :MLA_ROPE] + z[:, _E_KRS:_E_KRS + MLA_ROPE] * sin[:, :MLA_ROPE]
    q = _dot(cq, wuq_ref[...])
    qn_ref[...] = q[:, _Q_NOPE:_Q_ROPE].astype(_CD)
    qr_ref[...] = (q[:, _Q_ROPE:_Q_ROPES] * cos + q[:, _Q_ROPES:_Q_END] * sin).astype(_CD)
    ckv_ref[...] = ckv
    kr_ref[...] = kr
    ckvb_ref[...] = ckv.astype(_CD)
    krb_ref[...] = kr.astype(_CD)


def _even_proj(x, g, w_all, gq, gkv, wuq_all, cos_t, sin_t):
    n, d = x.shape
    tm = _tile(n, TOKEN_TILE)
    nblk = cos_t.shape[0] // tm
    const = lambda i: (0, 0)
    row = lambda i: (i, 0)
    tab = lambda i: (i % nblk, 0)
    widths = (POOL_DIM, MLA_HEADS * MLA_NOPE, _QR, MLA_KV_LORA, MLA_ROPE, MLA_KV_LORA, MLA_ROPE)
    dtypes = (_F32, _CD, _CD, _F32, _F32, _CD, _CD)
    return pl.pallas_call(
        _even_proj_body, grid=(n // tm,),
        in_specs=[pl.BlockSpec((tm, d), row), pl.BlockSpec((1, d), const), pl.BlockSpec(w_all.shape, const),
                  pl.BlockSpec((1, MLA_Q_LORA), const), pl.BlockSpec((1, MLA_KV_LORA), const),
                  pl.BlockSpec(wuq_all.shape, const), pl.BlockSpec((tm, _QR), tab), pl.BlockSpec((tm, _QR), tab)],
        out_specs=[pl.BlockSpec((tm, w), row) for w in widths],
        out_shape=[jax.ShapeDtypeStruct((n, w), dt) for w, dt in zip(widths, dtypes)],
        compiler_params=_cparams(1), name="even_proj",
    )(x, g.reshape(1, d), w_all, gq.reshape(1, -1), gkv.reshape(1, -1), wuq_all, cos_t, sin_t)


_HIST_PAD = 2 * SUBLANES


def _pool_prompt_body(u_ref, w_ref, sc_ref, y_ref, ext_ref, *, T):
    ext_ref[0:_HIST_PAD, :] = jnp.zeros((_HIST_PAD, POOL_DIM), _F32)
    ext_ref[_HIST_PAD:_HIST_PAD + T, :] = u_ref[...]
    pos = lax.broadcasted_iota(jnp.int32, (T, 1), 0)
    for g, w in enumerate(POOL_WINDOWS):
        sl = slice(g * POOL_GROUP_DIM, (g + 1) * POOL_GROUP_DIM)
        s = ext_ref[_HIST_PAD:_HIST_PAD + T, sl]
        for j in range(1, w):
            s = s + ext_ref[_HIST_PAD - j:_HIST_PAD - j + T, sl]
        cnt = jnp.minimum(pos + 1, w).astype(_F32)
        pooled = (s / cnt - u_ref[:, sl]).astype(_CD)
        y_ref[:, sl] = (_dot(pooled, w_ref[g]) * sc_ref[:, sl]).astype(y_ref.dtype)


def _pool_prompt(u, w_pool, scale, B, T):
    return pl.pallas_call(
        functools.partial(_pool_prompt_body, T=T), grid=(B,),
        in_specs=[pl.BlockSpec((T, POOL_DIM), lambda b: (b, 0)),
                  pl.BlockSpec(w_pool.shape, lambda b: (0, 0, 0)), pl.BlockSpec((1, POOL_DIM), lambda b: (0, 0))],
        out_specs=pl.BlockSpec((T, POOL_DIM), lambda b: (b, 0)),
        out_shape=jax.ShapeDtypeStruct((B * T, POOL_DIM), _CD),
        scratch_shapes=[pltpu.VMEM((_HIST_PAD + T, POOL_DIM), _F32)],
        compiler_params=_cparams(1), name="pool_prompt",
    )(u, w_pool, scale.reshape(1, POOL_DIM))


def _pool_sample_body(ext_ref, w_ref, sc_ref, y_ref, *, Td, pos0):
    for t in range(Td):
        for g, w in enumerate(POOL_WINDOWS):
            sl = slice(g * POOL_GROUP_DIM, (g + 1) * POOL_GROUP_DIM)
            s = ext_ref[POOL_STATE + t, :, sl]
            for j in range(1, w):
                s = s + ext_ref[POOL_STATE + t - j, :, sl]
            cnt = float(min(pos0 + t + 1, w))
            pooled = (s / cnt - ext_ref[POOL_STATE + t, :, sl]).astype(_CD)
            y_ref[t, :, sl] = (_dot(pooled, w_ref[g]) * sc_ref[:, sl]).astype(y_ref.dtype)


def _pool_sample(ext, w_pool, scale, Td, pos0):
    _, Bd, _ = ext.shape
    return pl.pallas_call(
        functools.partial(_pool_sample_body, Td=Td, pos0=pos0),
        out_shape=jax.ShapeDtypeStruct((Td, Bd, POOL_DIM), _CD),
        compiler_params=pltpu.CompilerParams(vmem_limit_bytes=VMEM_LIMIT), name="pool_sample",
    )(ext, w_pool, scale.reshape(1, POOL_DIM))


def _lanes(x, n):
    return x[:, :n] if n <= LANES else jnp.concatenate([x] * (n // LANES), axis=1)


def _fold_lanes(p):
    out = p[:, :LANES]
    for c in range(1, p.shape[1] // LANES):
        out = out + p[:, c * LANES:(c + 1) * LANES]
    return out


def _softmax_step(s, pv, m_ref, l_ref, acc_ref):
    m_prev = m_ref[...]
    m_new = jnp.maximum(m_prev, jnp.max(s, axis=-1, keepdims=True))
    alpha = jnp.exp(m_prev - m_new)
    p = jnp.exp(s - _lanes(m_new, s.shape[1]))
    l_ref[...] = alpha * l_ref[...] + _fold_lanes(p)
    acc_ref[...] = _lanes(alpha, acc_ref.shape[-1]) * acc_ref[...] + pv(p.astype(_CD))
    m_ref[...] = m_new


def _softmax_rescale(s, m_ref, l_ref, acc_ref, p_ref, first):
    m_new = jnp.max(s, axis=-1, keepdims=True)
    if first:
        m_new = jnp.broadcast_to(m_new, m_ref.shape)
    else:
        m_prev = m_ref[...]
        m_new = jnp.maximum(m_prev, m_new)
        alpha = jnp.exp(m_prev - m_new)
        acc_ref[...] = _lanes(alpha, acc_ref.shape[-1]) * acc_ref[...]
    p = jnp.exp(s - _lanes(m_new, s.shape[1]))
    if l_ref is not None:
        l_ref[...] = _fold_lanes(p) if first else alpha * l_ref[...] + _fold_lanes(p)
    p_ref[...] = p.astype(p_ref.dtype)
    m_ref[...] = m_new


def _mla_prompt_body(qn_ref, qr_ref, ckv_ref, kr_ref, wuk_ref, wuv_ref, o_ref, ql_s, qr_s, s_s, p_s, m_s, l_s, acc_s, *, TQ):
    i = pl.program_id(1)
    rows = [slice(h * TQ, (h + 1) * TQ) for h in range(MLA_HEADS)]
    for h in range(MLA_HEADS):
        ql_s[rows[h], :] = _dot(qn_ref[:, h * MLA_NOPE:(h + 1) * MLA_NOPE], wuk_ref[h]).astype(_CD)
        qr_s[rows[h], :] = qr_ref[:, h * MLA_ROPE:(h + 1) * MLA_ROPE]
    def block(off, visible, first):
        kc = ckv_ref[pl.ds(off, TQ), :]
        s_s[...] = (_dot_nt(ql_s[...], kc) + _dot_nt(qr_s[...], kr_ref[pl.ds(off, TQ), :])) * MLA_SCALE
        for r in rows:
            s = s_s[r, :]
            if visible is not None:
                s = jnp.where(visible, s, _NEG_INF)
            _softmax_rescale(s, m_s.at[r], l_s.at[r], acc_s.at[r], p_s.at[r], first)
        pv = _dot(p_s[...], kc)
        acc_s[...] = pv if first else acc_s[...] + pv

    tri = lax.broadcasted_iota(jnp.int32, (TQ, TQ), 1) <= lax.broadcasted_iota(jnp.int32, (TQ, TQ), 0)
    block(pl.multiple_of(i * TQ, TQ), tri, True)

    def body(j, c):
        block(pl.multiple_of(j * TQ, TQ), None, False)
        return c

    lax.fori_loop(0, i, body, 0)
    for h in range(MLA_HEADS):
        o_lat = (acc_s[rows[h], :] / jnp.sum(l_s[rows[h], :], axis=-1, keepdims=True)).astype(_CD)
        o_ref[:, h * MLA_V:(h + 1) * MLA_V] = _dot(o_lat, wuv_ref[h]).astype(o_ref.dtype)


def _mla_prompt(qn, qr, ckvb, krb, wuk, wuv, B, T):
    tq = _tile(T, ATTN_TILE)
    nq = T // tq
    R = MLA_HEADS * tq
    qmap = lambda b, i: (b * nq + i, 0)
    smap = lambda b, i: (b, 0)
    c3 = lambda b, i: (0, 0, 0)
    return pl.pallas_call(
        functools.partial(_mla_prompt_body, TQ=tq), grid=(B, nq),
        in_specs=[pl.BlockSpec((tq, qn.shape[1]), qmap), pl.BlockSpec((tq, qr.shape[1]), qmap),
                  pl.BlockSpec((T, MLA_KV_LORA), smap), pl.BlockSpec((T, MLA_ROPE), smap),
                  pl.BlockSpec(wuk.shape, c3), pl.BlockSpec(wuv.shape, c3)],
        out_specs=pl.BlockSpec((tq, MLA_HEADS * MLA_V), qmap),
        out_shape=jax.ShapeDtypeStruct((B * T, MLA_HEADS * MLA_V), _CD),
        scratch_shapes=[pltpu.VMEM((R, MLA_KV_LORA), _CD), pltpu.VMEM((R, MLA_ROPE), _CD),
                        pltpu.VMEM((R, tq), _F32), pltpu.VMEM((R, tq), _CD),
                        pltpu.VMEM((R, LANES), _F32), pltpu.VMEM((R, LANES), _F32), pltpu.VMEM((R, MLA_KV_LORA), _F32)],
        compiler_params=_cparams(2), name="mla_prompt",
    )(qn, qr, ckvb, krb, wuk, wuv)


def _mla_sample_body(pt_ref, qn_ref, qr_ref, nckv_ref, nkr_ref, wuk_ref, wuv_ref, *rest, PG, Td, TdP):
    ckv_pages, krt_pages = rest[:PG], rest[PG:2 * PG]
    o_ref, ql_s, qr_s, m_ref, l_ref, acc_ref = rest[2 * PG:]
    i = pl.program_id(1)
    R = MLA_HEADS * TdP

    @pl.when(i == 0)
    def _():
        for h in range(MLA_HEADS):
            ql_s[h * TdP:(h + 1) * TdP, :] = _dot(qn_ref[0][:, h * MLA_NOPE:(h + 1) * MLA_NOPE], wuk_ref[h])
            qr_s[h * TdP:(h + 1) * TdP, :] = qr_ref[0][:, h * MLA_ROPE:(h + 1) * MLA_ROPE].astype(_F32)
        m_ref[...] = jnp.full((R, LANES), _NEG_INF, _F32)
        l_ref[...] = jnp.zeros((R, LANES), _F32)
        acc_ref[...] = jnp.zeros((R, MLA_KV_LORA), _F32)

    ql = ql_s[...].astype(_CD)
    qr = qr_s[...].astype(_CD)
    kcs = [ckv_pages[j][...].astype(_CD) for j in range(PG)]
    s = jnp.concatenate([_dot_nt(ql, kcs[j]) + _dot(qr, krt_pages[j][...].astype(_CD)) for j in range(PG)], axis=1)

    def pv(p):
        out = _dot(p[:, :PAGE_SIZE], kcs[0])
        for j in range(1, PG):
            out = out + _dot(p[:, j * PAGE_SIZE:(j + 1) * PAGE_SIZE], kcs[j])
        return out

    _softmax_step(s * MLA_SCALE, pv, m_ref, l_ref, acc_ref)

    @pl.when(i == pl.num_programs(1) - 1)
    def _():
        kc = nckv_ref[0]
        s = (_dot_nt(ql, kc) + _dot_nt(qr, nkr_ref[0])) * MLA_SCALE
        t = lax.broadcasted_iota(jnp.int32, (R, PAGE_SIZE), 0) % TdP
        c = lax.broadcasted_iota(jnp.int32, (R, PAGE_SIZE), 1)
        s = jnp.where((c <= t) & (c < Td), s, _NEG_INF)
        _softmax_step(s, lambda p: _dot(p, kc), m_ref, l_ref, acc_ref)
        o_lat = (acc_ref[...] / jnp.sum(l_ref[...], axis=-1, keepdims=True)).astype(_CD)
        for h in range(MLA_HEADS):
            o_ref[0, :, h * MLA_V:(h + 1) * MLA_V] = _dot(o_lat[h * TdP:(h + 1) * TdP], wuv_ref[h])


def _page_spec(block, layer, PG, j):
    nd = len(block)

    def index_map(b, i, pt):
        return (layer, pt[b, i * PG + j]) + (0,) * (nd - 2)

    return pl.BlockSpec(block, index_map)


def _mla_sample(page_table, qn, qr, nckv, nkr, wuk, wuv, cache_ckv, cache_krt, layer, Td):
    Bd, TdP, _ = qn.shape
    n_pages = page_table.shape[1]
    pg = _tile(n_pages, PAGES_PER_STEP)
    seq = lambda b, i, pt: (b, 0, 0)
    c3 = lambda b, i, pt: (0, 0, 0)
    R = MLA_HEADS * TdP
    in_specs = [pl.BlockSpec((1, TdP, qn.shape[2]), seq), pl.BlockSpec((1, TdP, qr.shape[2]), seq),
                pl.BlockSpec((1, PAGE_SIZE, MLA_KV_LORA), seq), pl.BlockSpec((1, PAGE_SIZE, MLA_ROPE), seq),
                pl.BlockSpec(wuk.shape, c3), pl.BlockSpec(wuv.shape, c3)]
    in_specs += [_page_spec((None, None, PAGE_SIZE, MLA_KV_LORA), layer, pg, j) for j in range(pg)]
    in_specs += [_page_spec((None, None, MLA_ROPE, PAGE_SIZE), layer, pg, j) for j in range(pg)]
    return pl.pallas_call(
        functools.partial(_mla_sample_body, PG=pg, Td=Td, TdP=TdP),
        grid_spec=pltpu.PrefetchScalarGridSpec(
            num_scalar_prefetch=1, grid=(Bd, n_pages // pg), in_specs=in_specs,
            out_specs=pl.BlockSpec((1, TdP, MLA_HEADS * MLA_V), seq),
            scratch_shapes=[pltpu.VMEM((R, MLA_KV_LORA), _F32), pltpu.VMEM((R, MLA_ROPE), _F32),
                            pltpu.VMEM((R, LANES), _F32), pltpu.VMEM((R, LANES), _F32), pltpu.VMEM((R, MLA_KV_LORA), _F32)]),
        out_shape=jax.ShapeDtypeStruct((Bd, TdP, MLA_HEADS * MLA_V), _F32),
        compiler_params=_cparams(2), name="mla_sample",
    )(page_table, qn, qr, nckv, nkr, wuk, wuv, *([cache_ckv] * pg), *([cache_krt] * pg))


_O_Q, _O_K, _O_V, _O_QI, _O_KI, _O_WI, _O_END = 0, 1024, 1280, 1536, 2048, 2176, 2304
_KV = DSA_KV_HEADS * DSA_HEAD_DIM
assert all(math.frexp(s)[0] == 0.5 for s in (DSA_SCALE, IDX_SCALE))
_V1 = LANES


def _odd_proj_body(x_ref, g_ref, w_ref, q_ref, k_ref, v_ref, qi_ref, ki_ref, wi_ref, kb_ref, vb_ref, kib_ref, v1_ref):
    h = _rms(x_ref[...], g_ref[...]).astype(_CD)
    z = _dot(h, w_ref[...])
    q_ref[...] = (z[:, _O_Q:_O_K] * DSA_SCALE).astype(_CD)
    k = z[:, _O_K:_O_V]
    v = z[:, _O_V:_O_QI]
    ki = z[:, _O_KI:_O_KI + IDX_DIM]
    k_ref[...] = k
    v_ref[...] = v
    qi_ref[...] = (z[:, _O_QI:_O_KI] * IDX_SCALE).astype(_CD)
    ki_ref[...] = ki
    wi_ref[...] = z[:, _O_WI:_O_WI + IDX_HEADS] * (IDX_HEADS ** -0.5)
    kb_ref[...] = k.astype(_CD)
    vb_ref[...] = v.astype(_CD)
    kib_ref[...] = ki.astype(_CD)
    ones = jnp.ones((v.shape[0], _V1 - DSA_HEAD_DIM), _CD)
    for n in range(DSA_KV_HEADS):
        v1_ref[:, n * _V1:n * _V1 + DSA_HEAD_DIM] = v[:, n * DSA_HEAD_DIM:(n + 1) * DSA_HEAD_DIM].astype(_CD)
        v1_ref[:, n * _V1 + DSA_HEAD_DIM:(n + 1) * _V1] = ones


def _odd_proj(x, g, w_all):
    n, d = x.shape
    tm = _tile(n, TOKEN_TILE)
    const = lambda i: (0, 0)
    row = lambda i: (i, 0)
    widths = (DSA_HEADS * DSA_HEAD_DIM, _KV, _KV, IDX_HEADS * IDX_DIM, IDX_DIM, IDX_HEADS, _KV, _KV, IDX_DIM,
              DSA_KV_HEADS * _V1)
    dtypes = (_CD, _F32, _F32, _CD, _F32, _F32, _CD, _CD, _CD, _CD)
    return pl.pallas_call(
        _odd_proj_body, grid=(n // tm,),
        in_specs=[pl.BlockSpec((tm, d), row), pl.BlockSpec((1, d), const), pl.BlockSpec(w_all.shape, const)],
        out_specs=[pl.BlockSpec((tm, w), row) for w in widths],
        out_shape=[jax.ShapeDtypeStruct((n, w), dt) for w, dt in zip(widths, dtypes)],
        compiler_params=_cparams(1), name="odd_proj",
    )(x, g.reshape(1, d), w_all)


def _kth_largest_search(count, topk, idx_bits, shape):
    def value_bit(b, t_u):
        cand_u = t_u | lax.shift_left(jnp.int32(1), 31 - b)
        cand = cand_u ^ _MIN_I32
        return jnp.where(count(lambda k, idx: k >= cand) >= topk, cand_u, t_u)

    thr = lax.fori_loop(0, 32, value_bit, jnp.zeros(shape, jnp.int32)) ^ _MIN_I32
    need = topk - count(lambda k, idx: k > thr)
    n_ge = count(lambda k, idx: k >= thr)

    def index_bit(b, cut):
        cand = cut | lax.shift_left(jnp.int32(1), idx_bits - 1 - b)
        return jnp.where(count(lambda k, idx: (k == thr) & (idx < cand)) < need, cand, cut)

    cut = lax.cond(jnp.max(n_ge) > topk,
                   lambda: lax.fori_loop(0, idx_bits, index_bit, jnp.zeros(shape, jnp.int32)),
                   lambda: jnp.full(shape, (1 << idx_bits) - 1, jnp.int32))
    return thr, cut


def _selected(kk, idx, thr, cut):
    return (kk > thr) | ((kk == thr) & (idx <= cut))


def _dsa_prompt_body(q_ref, qi_ref, wi_ref, k_ref, v1_ref, ki_ref, o_ref, key_s, bias_s, wib_s, qs, s_s, p_s, m_s, acc_s,
                     *, TQ, T, topk):
    i = pl.program_id(1)
    rowpos = i * TQ + lax.broadcasted_iota(jnp.int32, (TQ, 1), 0)
    col0 = lax.broadcasted_iota(jnp.int32, (1, TQ), 1)
    for h in range(IDX_HEADS):
        wib_s[h] = jnp.broadcast_to(wi_ref[:, h:h + 1], (TQ, LANES))

    def select(nch):
        blocks = [slice(j * TQ, (j + 1) * TQ) for j in range(nch)]
        if nch * TQ <= topk:
            assert nch == 1
            bias_s[:, blocks[0]] = jnp.where(col0 <= rowpos, 0.0, _MASKED)
            return
        for j, js in enumerate(blocks):
            kic = ki_ref[js, :]
            isc = jnp.zeros((TQ, TQ), _F32)
            for h in range(IDX_HEADS):
                s = _dot_nt(qi_ref[:, h * IDX_DIM:(h + 1) * IDX_DIM], kic)
                isc = isc + jnp.maximum(s, 0.0) * _lanes(wib_s[h], TQ)
            if j == nch - 1:
                isc = jnp.where(j * TQ + col0 <= rowpos, isc, _NEG_INF)
            key_s[:, js] = _sort_key(isc)

        def count(pred):
            acc = None
            for j, js in enumerate(blocks):
                part = _fold_lanes(jnp.where(pred(key_s[:, js], j * TQ + col0), 1.0, 0.0))
                acc = part if acc is None else acc + part
            return jnp.sum(acc, axis=1, keepdims=True)

        thr, cut = _kth_largest_search(count, float(topk), max(1, (T - 1).bit_length()), (TQ, 1))
        for j, js in enumerate(blocks):
            sel = _selected(key_s[:, js], j * TQ + col0, thr, cut)
            if j == nch - 1:
                sel = sel & (j * TQ + col0 <= rowpos)
            bias_s[:, js] = jnp.where(sel, 0.0, _MASKED)

    for c in range(T // TQ):
        pl.when(i == c)(functools.partial(select, c + 1))

    G = DSA_GROUP * TQ
    for hh in range(DSA_HEADS):
        qs[hh * TQ:(hh + 1) * TQ, :] = q_ref[:, hh * DSA_HEAD_DIM:(hh + 1) * DSA_HEAD_DIM]

    def block(off, first):
        def kv_head(n):
            g = slice(n * G, (n + 1) * G)
            s_s[g, :] = _dot_nt(qs[g, :], k_ref[pl.ds(off, TQ), n * DSA_HEAD_DIM:(n + 1) * DSA_HEAD_DIM])
            for hh in range(n * DSA_GROUP, (n + 1) * DSA_GROUP):
                r = slice(hh * TQ, (hh + 1) * TQ)
                _softmax_rescale(s_s[r, :] + bias_s[:, pl.ds(off, TQ)], m_s.at[r], None, acc_s.at[r], p_s.at[r], first)
            pv = _dot(p_s[g, :], v1_ref[pl.ds(off, TQ), n * _V1:(n + 1) * _V1])
            acc_s[g, :] = pv if first else acc_s[g, :] + pv

        for n in range(DSA_KV_HEADS):
            pl.when(i >= 0)(functools.partial(kv_head, n))

    block(0, True)

    def body(j, c):
        block(pl.multiple_of(j * TQ, TQ), False)
        return c

    lax.fori_loop(1, i + 1, body, 0)
    for hh in range(DSA_HEADS):
        a = acc_s[hh * TQ:(hh + 1) * TQ, :]
        o_ref[:, hh * DSA_HEAD_DIM:(hh + 1) * DSA_HEAD_DIM] = (a[:, :DSA_HEAD_DIM] / a[:, _V1 - DSA_HEAD_DIM:]).astype(o_ref.dtype)


def _dsa_prompt(q, qi, wi, kb, v1, kib, B, T):
    tq = _tile(T, ATTN_TILE)
    nq = T // tq
    topk = min(TOPK_MAX, T // 4)
    qmap = lambda b, i: (b * nq + i, 0)
    smap = lambda b, i: (b, 0)
    return pl.pallas_call(
        functools.partial(_dsa_prompt_body, TQ=tq, T=T, topk=topk), grid=(B, nq),
        in_specs=[pl.BlockSpec((tq, q.shape[1]), qmap), pl.BlockSpec((tq, qi.shape[1]), qmap),
                  pl.BlockSpec((tq, IDX_HEADS), qmap),
                  pl.BlockSpec((T, _KV), smap), pl.BlockSpec((T, v1.shape[1]), smap), pl.BlockSpec((T, IDX_DIM), smap)],
        out_specs=pl.BlockSpec((tq, DSA_HEADS * DSA_HEAD_DIM), qmap),
        out_shape=jax.ShapeDtypeStruct((B * T, DSA_HEADS * DSA_HEAD_DIM), _CD),
        scratch_shapes=[pltpu.VMEM((tq, T), jnp.int32), pltpu.VMEM((tq, T), _F32),
                        pltpu.VMEM((IDX_HEADS, tq, LANES), _F32), pltpu.VMEM((DSA_HEADS * tq, DSA_HEAD_DIM), _CD),
                        pltpu.VMEM((DSA_HEADS * tq, tq), _F32), pltpu.VMEM((DSA_HEADS * tq, tq), _CD),
                        pltpu.VMEM((DSA_HEADS * tq, LANES), _F32), pltpu.VMEM((DSA_HEADS * tq, _V1), _F32)],
        compiler_params=_cparams(2), name="dsa_prompt",
    )(q, qi, wi, kb, v1, kib)


def _dsa_keys_body(pt_ref, qi_ref, wi_ref, nki_ref, *rest, PG, P, Td, TdP):
    kit_pages = rest[:PG]
    key_ref, qs, ws = rest[PG:]
    i = pl.program_id(1)

    @pl.when(i == 0)
    def _():
        for h in range(IDX_HEADS):
            qs[h * TdP:(h + 1) * TdP, :] = qi_ref[0][:, h * IDX_DIM:(h + 1) * IDX_DIM].astype(_F32)
            ws[h * TdP:(h + 1) * TdP, :] = jnp.broadcast_to(wi_ref[0][:, h:h + 1], (TdP, PAGE_SIZE))

    def index_scores(qk):
        s = jnp.maximum(qk, 0.0) * ws[...]
        isc = s[0:TdP]
        for h in range(1, IDX_HEADS):
            isc = isc + s[h * TdP:(h + 1) * TdP]
        return isc

    qb = qs[...].astype(_CD)
    for j in range(PG):
        off = pl.multiple_of((i * PG + j) * PAGE_SIZE, PAGE_SIZE)
        key_ref[0, :, pl.ds(off, PAGE_SIZE)] = _sort_key(index_scores(_dot(qb, kit_pages[j][...].astype(_CD))))

    @pl.when(i == pl.num_programs(1) - 1)
    def _():
        isc = index_scores(_dot_nt(qb, nki_ref[0]))
        t = lax.broadcasted_iota(jnp.int32, (TdP, PAGE_SIZE), 0)
        c = lax.broadcasted_iota(jnp.int32, (TdP, PAGE_SIZE), 1)
        key_ref[0, :, P:P + PAGE_SIZE] = _sort_key(jnp.where((c <= t) & (c < Td), isc, _NEG_INF))


def _dsa_keys(page_table, qi, wi, nki, cache_kit, layer, Td):
    Bd, TdP, _ = qi.shape
    n_pages = page_table.shape[1]
    P = n_pages * PAGE_SIZE
    pg = _tile(n_pages, PAGES_PER_STEP)
    seq = lambda b, i, pt: (b, 0, 0)
    R = IDX_HEADS * TdP
    in_specs = [pl.BlockSpec((1, TdP, qi.shape[2]), seq), pl.BlockSpec((1, TdP, IDX_HEADS), seq),
                pl.BlockSpec((1, PAGE_SIZE, IDX_DIM), seq)]
    in_specs += [_page_spec((None, None, IDX_DIM, PAGE_SIZE), layer, pg, j) for j in range(pg)]
    return pl.pallas_call(
        functools.partial(_dsa_keys_body, PG=pg, P=P, Td=Td, TdP=TdP),
        grid_spec=pltpu.PrefetchScalarGridSpec(
            num_scalar_prefetch=1, grid=(Bd, n_pages // pg), in_specs=in_specs,
            out_specs=pl.BlockSpec((1, TdP, P + PAGE_SIZE), seq),
            scratch_shapes=[pltpu.VMEM((R, IDX_DIM), _F32), pltpu.VMEM((R, PAGE_SIZE), _F32)]),
        out_shape=jax.ShapeDtypeStruct((Bd, TdP, P + PAGE_SIZE), jnp.int32),
        compiler_params=_cparams(2), name="dsa_keys",
    )(page_table, qi, wi, nki, *([cache_kit] * pg))


def _dsa_threshold_body(key_ref, thr_ref, cut_ref, *, topk):
    rows, width = key_ref.shape
    idx = lax.broadcasted_iota(jnp.int32, (1, width), 1)

    def count(pred):
        hit = jnp.where(pred(key_ref[...], idx), 1.0, 0.0)
        parts = [hit[:, c * LANES:(c + 1) * LANES] for c in range(width // LANES)]
        while len(parts) > 1:
            parts = [a + b for a, b in zip(parts[::2], parts[1::2])] + ([parts[-1]] if len(parts) % 2 else [])
        return jnp.sum(parts[0], axis=1, keepdims=True)

    thr, cut = _kth_largest_search(count, float(topk), (width - 1).bit_length(), (rows, 1))
    thr_ref[...] = jnp.broadcast_to(thr, (rows, LANES))
    cut_ref[...] = jnp.broadcast_to(cut, (rows, LANES))


def _dsa_threshold(keys, topk):
    rows, width = keys.shape
    rb = _tile(rows, SEARCH_ROWS)
    row = lambda i: (i, 0)
    return pl.pallas_call(
        functools.partial(_dsa_threshold_body, topk=topk), grid=(rows // rb,),
        in_specs=[pl.BlockSpec((rb, width), row)],
        out_specs=[pl.BlockSpec((rb, LANES), row), pl.BlockSpec((rb, LANES), row)],
        out_shape=[jax.ShapeDtypeStruct((rows, LANES), jnp.int32)] * 2,
        compiler_params=_cparams(1), name="dsa_threshold",
    )(keys)


def _dsa_sample_body(pt_ref, q_ref, key_ref, nkey_ref, thr_ref, cut_ref, nk_ref, nv_ref, *rest, PG, P, Td, TdP):
    kt_pages, vt_pages = rest[:PG], rest[PG:2 * PG]
    o_ref, qs, m_ref, l_ref, acc_ref = rest[2 * PG:]
    i = pl.program_id(1)
    RH = DSA_GROUP * TdP
    R = DSA_KV_HEADS * RH
    heads = [slice(n * RH, (n + 1) * RH) for n in range(DSA_KV_HEADS)]

    @pl.when(i == 0)
    def _():
        for hh in range(DSA_HEADS):
            qs[hh * TdP:(hh + 1) * TdP, :] = q_ref[0][:, hh * DSA_HEAD_DIM:(hh + 1) * DSA_HEAD_DIM].astype(_F32)
        m_ref[...] = jnp.full((R, LANES), _MASKED, _F32)
        l_ref[...] = jnp.zeros((R, LANES), _F32)
        acc_ref[...] = jnp.zeros((R, DSA_HEAD_DIM), _F32)

    thr = thr_ref[0][:, 0:1]
    cut = cut_ref[0][:, 0:1]
    qb = qs[...].astype(_CD)

    def attend(kk, idx, visible, scores, pv):
        sel = _selected(kk, idx, thr, cut)
        if visible is not None:
            sel = sel & visible
        bias = jnp.concatenate([jnp.where(sel, 0.0, _MASKED)] * (R // TdP), axis=0)
        s = jnp.concatenate([scores(n, qb[heads[n]]) for n in range(DSA_KV_HEADS)], axis=0) + bias
        _softmax_step(s, lambda p: jnp.concatenate([pv(n, p[heads[n]]) for n in range(DSA_KV_HEADS)], axis=0),
                      m_ref, l_ref, acc_ref)

    kts = [kt_pages[j][...].astype(_CD) for j in range(PG)]
    vts = [vt_pages[j][...].astype(_CD) for j in range(PG)]
    W = PG * PAGE_SIZE
    idx = i * W + lax.broadcasted_iota(jnp.int32, (TdP, W), 1)

    def past_pv(n, p):
        out = _dot_nt(p[:, :PAGE_SIZE], vts[0][n])
        for j in range(1, PG):
            out = out + _dot_nt(p[:, j * PAGE_SIZE:(j + 1) * PAGE_SIZE], vts[j][n])
        return out

    attend(key_ref[0], idx, None,
           lambda n, q: jnp.concatenate([_dot(q, kts[j][n]) for j in range(PG)], axis=1), past_pv)

    @pl.when(i == pl.num_programs(1) - 1)
    def _():
        t = lax.broadcasted_iota(jnp.int32, (TdP, PAGE_SIZE), 0)
        c = lax.broadcasted_iota(jnp.int32, (TdP, PAGE_SIZE), 1)
        attend(nkey_ref[0], P + c, (c <= t) & (c < Td),
               lambda n, q: _dot_nt(q, nk_ref[0][:, n * DSA_HEAD_DIM:(n + 1) * DSA_HEAD_DIM]),
               lambda n, p: _dot(p, nv_ref[0][:, n * DSA_HEAD_DIM:(n + 1) * DSA_HEAD_DIM]))
        o = acc_ref[...] / jnp.sum(l_ref[...], axis=-1, keepdims=True)
        for hh in range(DSA_HEADS):
            o_ref[0, :, hh * DSA_HEAD_DIM:(hh + 1) * DSA_HEAD_DIM] = o[hh * TdP:(hh + 1) * TdP]


def _dsa_sample(page_table, q, keys, nkeys, thr, cut, nk, nv, cache_kt, cache_vt, layer, Td):
    Bd, TdP, _ = q.shape
    n_pages = page_table.shape[1]
    P = n_pages * PAGE_SIZE
    pg = _tile(n_pages, PAGES_PER_STEP)
    seq = lambda b, i, pt: (b, 0, 0)
    R = DSA_HEADS * TdP
    in_specs = [pl.BlockSpec((1, TdP, q.shape[2]), seq),
                pl.BlockSpec((1, TdP, pg * PAGE_SIZE), lambda b, i, pt: (b, 0, i)),
                pl.BlockSpec((1, TdP, PAGE_SIZE), seq), pl.BlockSpec((1, TdP, LANES), seq),
                pl.BlockSpec((1, TdP, LANES), seq),
                pl.BlockSpec((1, PAGE_SIZE, _KV), seq), pl.BlockSpec((1, PAGE_SIZE, _KV), seq)]
    page_block = (None, None, DSA_KV_HEADS, DSA_HEAD_DIM, PAGE_SIZE)
    in_specs += [_page_spec(page_block, layer, pg, j) for j in range(pg)]
    in_specs += [_page_spec(page_block, layer, pg, j) for j in range(pg)]
    return pl.pallas_call(
        functools.partial(_dsa_sample_body, PG=pg, P=P, Td=Td, TdP=TdP),
        grid_spec=pltpu.PrefetchScalarGridSpec(
            num_scalar_prefetch=1, grid=(Bd, n_pages // pg), in_specs=in_specs,
            out_specs=pl.BlockSpec((1, TdP, DSA_HEADS * DSA_HEAD_DIM), seq),
            scratch_shapes=[pltpu.VMEM((R, DSA_HEAD_DIM), _F32), pltpu.VMEM((R, LANES), _F32),
                            pltpu.VMEM((R, LANES), _F32), pltpu.VMEM((R, DSA_HEAD_DIM), _F32)]),
        out_shape=jax.ShapeDtypeStruct((Bd, TdP, DSA_HEADS * DSA_HEAD_DIM), _F32),
        compiler_params=_cparams(2), name="dsa_sample",
    )(page_table, q, keys, nkeys, thr, cut, nk, nv, *([cache_kt] * pg), *([cache_vt] * pg))


def _rope_tables(pos):
    half = MLA_ROPE // 2
    inv = ROPE_THETA ** (-jnp.arange(half, dtype=_F32) / half)
    ang = pos.astype(_F32)[:, None] * inv[None, :]
    cos, sin = jnp.cos(ang), jnp.sin(ang)
    return (jnp.tile(jnp.concatenate([cos, cos], axis=1), (1, MLA_HEADS)),
            jnp.tile(jnp.concatenate([-sin, sin], axis=1), (1, MLA_HEADS)))


def _pack_cols(w, pieces, total):
    out = jnp.zeros((w.shape[0], total), w.dtype)
    for start, cols in pieces:
        out = out.at[:, start:start + len(cols)].set(w[:, np.asarray(cols)])
    return out.astype(_CD)


def _swap_halves(n):
    return (np.arange(n) + n // 2) % n


def _pad_rows(a, rows):
    return jnp.pad(a, ((0, 0), (0, rows - a.shape[1]), (0, 0)))


def kernel(x_prompt, x_sample, cache_mla_ckv, cache_mla_kr, cache_dsa_k, cache_dsa_v, cache_dsa_kidx, state_pool, page_table, ffn_norm, ffn_w1, ffn_w3, ffn_w2, mix_norm, final_norm, even_w_in, mla_q_norm, mla_kv_norm, mla_w_uq, mla_w_uk, mla_w_uv, pool_w, pool_scale, even_w_out, odd_w_in, odd_w_out):
    B, T, D = x_prompt.shape
    Bd, Td, _ = x_sample.shape
    depth = ffn_norm.shape[0]
    P = page_table.shape[1] * PAGE_SIZE
    TdP = -(-Td // SUBLANES) * SUBLANES
    assert D == D_MODEL and Td <= PAGE_SIZE and T >= POOL_STATE
    Np, Ns = B * T, Bd * Td
    tm_s = _tile(Ns, TOKEN_TILE)
    assert tm_s % Td == 0

    pos_p = jnp.arange(T, dtype=jnp.int32)
    pos_s = P + jnp.arange(Td, dtype=jnp.int32)
    cos_p, sin_p = _rope_tables(pos_p)
    cos_s, sin_s = (jnp.tile(a, (tm_s // Td, 1)) for a in _rope_tables(pos_s))

    def seq_rows(a, rows=TdP):
        return _pad_rows(a.reshape(Bd, Td, a.shape[1]), rows)

    cache_krt = jnp.swapaxes(cache_mla_kr, 2, 3)
    cache_kit = jnp.swapaxes(cache_dsa_kidx, 2, 3)
    cache_kt = jnp.transpose(cache_dsa_k, (0, 1, 3, 4, 2))
    cache_vt = jnp.transpose(cache_dsa_v, (0, 1, 3, 4, 2))

    xp = x_prompt.reshape(Np, D)
    xs = x_sample.reshape(Ns, D)
    pre_p, pre_s = (), ()
    outs = {k: [] for k in ("p_ckv", "p_kr", "p_k", "p_v", "p_ki", "p_pool", "s_ckv", "s_kr", "s_k", "s_v", "s_ki", "s_pool")}
    ffn_w = [w.astype(_CD) for w in (ffn_w1, ffn_w3, ffn_w2)]
    for l in range(depth):
        j = l // 2
        wa = [w[l, 0] for w in ffn_w]
        wb = [w[l, 1] for w in ffn_w]
        xp = _ffn(xp, ffn_norm[l, 0], *wa, pre=pre_p)
        xs = _ffn(xs, ffn_norm[l, 0], *wa, pre=pre_s)
        if l % 2 == 0:
            kr0 = POOL_DIM + MLA_Q_LORA + MLA_KV_LORA
            w_all = _pack_cols(even_w_in[j], [(_E_U, np.arange(kr0)), (_E_KR, kr0 + np.arange(MLA_ROPE)),
                                              (_E_KRS, kr0 + _swap_halves(MLA_ROPE))], _E_END)
            per_head = MLA_NOPE + MLA_ROPE
            heads = np.arange(MLA_HEADS)[:, None] * per_head
            wuq_all = _pack_cols(mla_w_uq[j], [(_Q_NOPE, (heads + np.arange(MLA_NOPE)[None]).ravel()),
                                               (_Q_ROPE, (heads + MLA_NOPE + np.arange(MLA_ROPE)[None]).ravel()),
                                               (_Q_ROPES, (heads + MLA_NOPE + _swap_halves(MLA_ROPE)[None]).ravel())], _Q_END)
            wuk = jnp.transpose(mla_w_uk[j], (1, 2, 0)).astype(_CD)
            wuv = jnp.transpose(mla_w_uv[j], (1, 0, 2)).astype(_CD)
            w_out = even_w_out[j].astype(_CD)
            pw = pool_w[j].astype(_CD)
            u, qn, qr, ckv, kr, ckvb, krb = _even_proj(xp, mix_norm[l], w_all, mla_q_norm[j], mla_kv_norm[j], wuq_all, cos_p, sin_p)
            y_pool = _pool_prompt(u, pw, pool_scale[j], B, T)
            o = _mla_prompt(qn, qr, ckvb, krb, wuk, wuv, B, T)
            pre_p = ((y_pool, w_out[:POOL_DIM]), (o, w_out[POOL_DIM:]))
            outs["p_ckv"].append(ckv.reshape(B, T, MLA_KV_LORA))
            outs["p_kr"].append(kr.reshape(B, T, MLA_ROPE))
            outs["p_pool"].append(u.reshape(B, T, POOL_DIM)[:, T - POOL_STATE:])
            u, qn, qr, ckv, kr, ckvb, krb = _even_proj(xs, mix_norm[l], w_all, mla_q_norm[j], mla_kv_norm[j], wuq_all, cos_s, sin_s)
            ext = jnp.concatenate([state_pool[j], u.reshape(Bd, Td, POOL_DIM)], axis=1)
            y_pool = _pool_sample(jnp.swapaxes(ext, 0, 1), pw, pool_scale[j], Td, P)
            y_pool = jnp.swapaxes(y_pool, 0, 1).reshape(Ns, POOL_DIM)
            o = _mla_sample(page_table, seq_rows(qn), seq_rows(qr), seq_rows(ckvb, PAGE_SIZE), seq_rows(krb, PAGE_SIZE),
                            wuk, wuv, cache_mla_ckv, cache_krt, j, Td)
            o = o[:, :Td].reshape(Ns, MLA_HEADS * MLA_V).astype(_CD)
            pre_s = ((y_pool, w_out[:POOL_DIM]), (o, w_out[POOL_DIM:]))
            outs["s_ckv"].append(ckv.reshape(Bd, Td, MLA_KV_LORA))
            outs["s_kr"].append(kr.reshape(Bd, Td, MLA_ROPE))
            outs["s_pool"].append(ext[:, -POOL_STATE:])
        else:
            o5 = (DSA_HEADS + 2 * DSA_KV_HEADS) * DSA_HEAD_DIM + IDX_HEADS * IDX_DIM + IDX_DIM
            o4 = o5 - IDX_DIM
            w_all = _pack_cols(odd_w_in[j], [(_O_Q, np.arange(o4)), (_O_KI, o4 + np.arange(IDX_DIM)),
                                             (_O_WI, o5 + np.arange(IDX_HEADS))], _O_END)
            w_out = odd_w_out[j].astype(_CD)
            q, k, v, qi, ki, wi, kb, vb, kib, v1 = _odd_proj(xp, mix_norm[l], w_all)
            o = _dsa_prompt(q, qi, wi, kb, v1, kib, B, T)
            pre_p = ((o, w_out),)
            outs["p_k"].append(k.reshape(B, T, DSA_KV_HEADS, DSA_HEAD_DIM))
            outs["p_v"].append(v.reshape(B, T, DSA_KV_HEADS, DSA_HEAD_DIM))
            outs["p_ki"].append(ki.reshape(B, T, IDX_DIM))
            q, k, v, qi, ki, wi, kb, vb, kib, _ = _odd_proj(xs, mix_norm[l], w_all)
            keys = _dsa_keys(page_table, seq_rows(qi), seq_rows(wi), seq_rows(kib, PAGE_SIZE), cache_kit, j, Td)
            thr, cut = _dsa_threshold(keys[:, :Td].reshape(Ns, P + PAGE_SIZE), min(TOPK_MAX, (P + Td) // 4))
            o = _dsa_sample(page_table, seq_rows(q), keys, keys[:, :, P:], seq_rows(thr), seq_rows(cut), seq_rows(kb, PAGE_SIZE),
                            seq_rows(vb, PAGE_SIZE), cache_kt, cache_vt, j, Td)
            o = o[:, :Td].reshape(Ns, DSA_HEADS * DSA_HEAD_DIM).astype(_CD)
            pre_s = ((o, w_out),)
            outs["s_k"].append(k.reshape(Bd, Td, DSA_KV_HEADS, DSA_HEAD_DIM))
            outs["s_v"].append(v.reshape(Bd, Td, DSA_KV_HEADS, DSA_HEAD_DIM))
            outs["s_ki"].append(ki.reshape(Bd, Td, IDX_DIM))
        last = l == depth - 1
        xp = _ffn(xp, ffn_norm[l, 1], *wb, pre=pre_p, final_g=final_norm if last else None)
        xs = _ffn(xs, ffn_norm[l, 1], *wb, pre=pre_s, final_g=final_norm if last else None)
        pre_p, pre_s = (), ()
    st = {k: jnp.stack(v) for k, v in outs.items()}
    return (xp.reshape(B, T, D), xs.reshape(Bd, Td, D),
            st["p_ckv"], st["p_kr"], st["p_k"], st["p_v"], st["p_ki"], st["p_pool"],
            st["s_ckv"], st["s_kr"], st["s_k"], st["s_v"], st["s_ki"], st["s_pool"])
```

```python
import functools
import math

import numpy as np
import jax
import jax.numpy as jnp
from jax import lax
from jax.experimental import pallas as pl
from jax.experimental.pallas import tpu as pltpu

D_MODEL = 1024
D_FF = 2 * D_MODEL
EPS = 1e-6
PAGE_SIZE = 128
POOL_WINDOWS = (2, 4, 8, 16)
POOL_GROUPS = len(POOL_WINDOWS)
POOL_GROUP_DIM = D_MODEL // 8
POOL_DIM = POOL_GROUPS * POOL_GROUP_DIM
POOL_STATE = max(POOL_WINDOWS) - 1
MLA_HEADS = 8
MLA_NOPE = D_MODEL // 16
MLA_ROPE = D_MODEL // 32
MLA_V = D_MODEL // 16
MLA_Q_LORA = 3 * D_MODEL // 8
MLA_KV_LORA = D_MODEL // 4
MLA_SCALE = (MLA_NOPE + MLA_ROPE) ** -0.5
ROPE_THETA = 10000.0
DSA_HEADS = 16
DSA_KV_HEADS = 4
DSA_GROUP = DSA_HEADS // DSA_KV_HEADS
DSA_HEAD_DIM = D_MODEL // 16
DSA_SCALE = DSA_HEAD_DIM ** -0.5
IDX_HEADS = 8
IDX_DIM = D_MODEL // 16
IDX_SCALE = IDX_DIM ** -0.5
TOPK_MAX = 256

LANES = 128
SUBLANES = 8
VMEM_LIMIT = 56 * 1024 * 1024

TOKEN_TILE = 512
FF_CHUNK = 512
ATTN_TILE = 256
SEARCH_ROWS = 64
PAGES_PER_STEP = 32

_CD = jnp.bfloat16
_F32 = jnp.float32
_NT = (((1,), (1,)), ((), ()))
_MIN_I32 = np.int32(-2 ** 31)
_MASKED = -1e30
_NEG_INF = float("-inf")


def _cparams(n_grid):
    return pltpu.CompilerParams(dimension_semantics=("arbitrary",) * n_grid, vmem_limit_bytes=VMEM_LIMIT)


def _dot(a, b):
    return jnp.dot(a, b, preferred_element_type=_F32)


def _dot_nt(a, b):
    return lax.dot_general(a, b, _NT, preferred_element_type=_F32)


def _rms(x, g):
    return x * lax.rsqrt(jnp.mean(x * x, axis=-1, keepdims=True) + EPS) * g


def _sort_key(x):
    bits = lax.bitcast_convert_type(x, jnp.int32)
    return jnp.where(bits < 0, bits ^ np.int32(0x7FFFFFFF), bits)


def _tile(n, cap):
    t = min(cap, n)
    assert n % t == 0, (n, t)
    return t


def _ffn_body(*refs, n_pre, final):
    x_ref, g_ref, w1_ref, w3_ref, w2_ref = refs[:5]
    pre = refs[5:5 + 2 * n_pre]
    rest = refs[5 + 2 * n_pre:]
    o_ref = rest[-1]
    x = x_ref[...]
    if n_pre:
        d = _dot(pre[0][...], pre[1][...])
        for k in range(1, n_pre):
            d = d + _dot(pre[2 * k][...], pre[2 * k + 1][...])
        x = x + d
    h = _rms(x, g_ref[...]).astype(_CD)
    acc = jnp.zeros_like(x)
    for c in range(D_FF // FF_CHUNK):
        sl = slice(c * FF_CHUNK, (c + 1) * FF_CHUNK)
        a = _dot(h, w1_ref[:, sl])
        b = _dot(h, w3_ref[:, sl])
        acc = acc + _dot((a * jax.nn.sigmoid(a) * b).astype(_CD), w2_ref[sl, :])
    y = x + 0.5 * acc
    if final:
        y = _rms(y, rest[0][...])
    o_ref[...] = y


def _ffn(x, g, w1, w3, w2, pre=(), final_g=None):
    n, d = x.shape
    tm = _tile(n, TOKEN_TILE)
    const = lambda i: (0, 0)
    row = lambda i: (i, 0)
    in_specs = [pl.BlockSpec((tm, d), row), pl.BlockSpec((1, d), const),
                pl.BlockSpec(w1.shape, const), pl.BlockSpec(w3.shape, const), pl.BlockSpec(w2.shape, const)]
    args = [x, g.reshape(1, d), w1, w3, w2]
    for a, w in pre:
        in_specs += [pl.BlockSpec((tm, a.shape[1]), row), pl.BlockSpec(w.shape, const)]
        args += [a, w]
    if final_g is not None:
        in_specs.append(pl.BlockSpec((1, d), const))
        args.append(final_g.reshape(1, d))
    return pl.pallas_call(
        functools.partial(_ffn_body, n_pre=len(pre), final=final_g is not None),
        grid=(n // tm,), in_specs=in_specs, out_specs=pl.BlockSpec((tm, d), row),
        out_shape=jax.ShapeDtypeStruct((n, d), _F32), compiler_params=_cparams(1), name="ffn_half",
    )(*args)


_E_U, _E_CQ, _E_CKV, _E_KR, _E_KRS, _E_END = 0, 512, 896, 1152, 1280, 1408
_Q_NOPE, _Q_ROPE, _Q_ROPES, _Q_END = 0, 512, 768, 1024
_QR = MLA_HEADS * MLA_ROPE


def _even_proj_body(x_ref, g_ref, w_ref, gq_ref, gkv_ref, wuq_ref, cos_ref, sin_ref,
                    u_ref, qn_ref, qr_ref, ckv_ref, kr_ref, ckvb_ref, krb_ref, *, time_minor):
    h = _rms(x_ref[...], g_ref[...]).astype(_CD)
    z = _dot(h, w_ref[...])
    u_ref[...] = z[:, _E_U:_E_CQ]
    cq = _rms(z[:, _E_CQ:_E_CKV], gq_ref[...]).astype(_CD)
    ckv = _rms(z[:, _E_CKV:_E_KR], gkv_ref[...])
    cos = cos_ref[...]
    sin = sin_ref[...]
    kr = z[:, _E_KR:_E_KR + MLA_ROPE] * cos[:, :MLA_ROPE] + z[:, _E_KRS:_E_KRS + MLA_ROPE] * sin[:, :MLA_ROPE]
    q = _dot(cq, wuq_ref[...])
    qn_ref[...] = q[:, _Q_NOPE:_Q_ROPE].astype(_CD)
    qr_ref[...] = (q[:, _Q_ROPE:_Q_ROPES] * cos + q[:, _Q_ROPES:_Q_END] * sin).astype(_CD)
    ckv_ref[...] = ckv
    if time_minor:
        kr_ref[0] = kr.T
    else:
        kr_ref[...] = kr
    ckvb_ref[...] = ckv.astype(_CD)
    krb_ref[...] = kr.astype(_CD)


def _time_minor_out(n, tm, seq_len, width):
    per_seq = seq_len // tm
    return (pl.BlockSpec((1, width, tm), lambda i: (i // per_seq, 0, i % per_seq)),
            jax.ShapeDtypeStruct((n // seq_len, width, seq_len), _F32))


def _even_proj(x, g, w_all, gq, gkv, wuq_all, cos_t, sin_t, seq_len=None):
    n, d = x.shape
    tm = _tile(n, TOKEN_TILE)
    nblk = cos_t.shape[0] // tm
    const = lambda i: (0, 0)
    row = lambda i: (i, 0)
    tab = lambda i: (i % nblk, 0)
    widths = (POOL_DIM, MLA_HEADS * MLA_NOPE, _QR, MLA_KV_LORA, MLA_ROPE, MLA_KV_LORA, MLA_ROPE)
    dtypes = (_F32, _CD, _CD, _F32, _F32, _CD, _CD)
    out_specs = [pl.BlockSpec((tm, w), row) for w in widths]
    out_shape = [jax.ShapeDtypeStruct((n, w), dt) for w, dt in zip(widths, dtypes)]
    if seq_len is not None:
        out_specs[4], out_shape[4] = _time_minor_out(n, tm, seq_len, MLA_ROPE)
    return pl.pallas_call(
        functools.partial(_even_proj_body, time_minor=seq_len is not None), grid=(n // tm,),
        in_specs=[pl.BlockSpec((tm, d), row), pl.BlockSpec((1, d), const), pl.BlockSpec(w_all.shape, const),
                  pl.BlockSpec((1, MLA_Q_LORA), const), pl.BlockSpec((1, MLA_KV_LORA), const),
                  pl.BlockSpec(wuq_all.shape, const), pl.BlockSpec((tm, _QR), tab), pl.BlockSpec((tm, _QR), tab)],
        out_specs=out_specs, out_shape=out_shape,
        compiler_params=_cparams(1), name="even_proj",
    )(x, g.reshape(1, d), w_all, gq.reshape(1, -1), gkv.reshape(1, -1), wuq_all, cos_t, sin_t)


_HIST_PAD = 2 * SUBLANES


def _pool_prompt_body(u_ref, w_ref, sc_ref, y_ref, ext_ref, *, T):
    ext_ref[0:_HIST_PAD, :] = jnp.zeros((_HIST_PAD, POOL_DIM), _F32)
    ext_ref[_HIST_PAD:_HIST_PAD + T, :] = u_ref[...]
    pos = lax.broadcasted_iota(jnp.int32, (T, 1), 0)
    for g, w in enumerate(POOL_WINDOWS):
        sl = slice(g * POOL_GROUP_DIM, (g + 1) * POOL_GROUP_DIM)
        s = ext_ref[_HIST_PAD:_HIST_PAD + T, sl]
        for j in range(1, w):
            s = s + ext_ref[_HIST_PAD - j:_HIST_PAD - j + T, sl]
        cnt = jnp.minimum(pos + 1, w).astype(_F32)
        pooled = (s / cnt - u_ref[:, sl]).astype(_CD)
        y_ref[:, sl] = (_dot(pooled, w_ref[g]) * sc_ref[:, sl]).astype(y_ref.dtype)


def _pool_prompt(u, w_pool, scale, B, T):
    return pl.pallas_call(
        functools.partial(_pool_prompt_body, T=T), grid=(B,),
        in_specs=[pl.BlockSpec((T, POOL_DIM), lambda b: (b, 0)),
                  pl.BlockSpec(w_pool.shape, lambda b: (0, 0, 0)), pl.BlockSpec((1, POOL_DIM), lambda b: (0, 0))],
        out_specs=pl.BlockSpec((T, POOL_DIM), lambda b: (b, 0)),
        out_shape=jax.ShapeDtypeStruct((B * T, POOL_DIM), _CD),
        scratch_shapes=[pltpu.VMEM((_HIST_PAD + T, POOL_DIM), _F32)],
        compiler_params=_cparams(1), name="pool_prompt",
    )(u, w_pool, scale.reshape(1, POOL_DIM))


def _pool_sample_body(ext_ref, w_ref, sc_ref, y_ref, *, Td, pos0):
    for t in range(Td):
        for g, w in enumerate(POOL_WINDOWS):
            sl = slice(g * POOL_GROUP_DIM, (g + 1) * POOL_GROUP_DIM)
            s = ext_ref[POOL_STATE + t, :, sl]
            for j in range(1, w):
                s = s + ext_ref[POOL_STATE + t - j, :, sl]
            cnt = float(min(pos0 + t + 1, w))
            pooled = (s / cnt - ext_ref[POOL_STATE + t, :, sl]).astype(_CD)
            y_ref[t, :, sl] = (_dot(pooled, w_ref[g]) * sc_ref[:, sl]).astype(y_ref.dtype)


def _pool_sample(ext, w_pool, scale, Td, pos0):
    _, Bd, _ = ext.shape
    return pl.pallas_call(
        functools.partial(_pool_sample_body, Td=Td, pos0=pos0),
        out_shape=jax.ShapeDtypeStruct((Td, Bd, POOL_DIM), _CD),
        compiler_params=pltpu.CompilerParams(vmem_limit_bytes=VMEM_LIMIT), name="pool_sample",
    )(ext, w_pool, scale.reshape(1, POOL_DIM))


def _lanes(x, n):
    return x[:, :n] if n <= LANES else jnp.concatenate([x] * (n // LANES), axis=1)


def _fold_lanes(p):
    out = p[:, :LANES]
    for c in range(1, p.shape[1] // LANES):
        out = out + p[:, c * LANES:(c + 1) * LANES]
    return out


def _softmax_step(s, pv, m_ref, l_ref, acc_ref):
    m_prev = m_ref[...]
    m_new = jnp.maximum(m_prev, jnp.max(s, axis=-1, keepdims=True))
    alpha = jnp.exp(m_prev - m_new)
    p = jnp.exp(s - _lanes(m_new, s.shape[1]))
    l_ref[...] = alpha * l_ref[...] + _fold_lanes(p)
    acc_ref[...] = _lanes(alpha, acc_ref.shape[-1]) * acc_ref[...] + pv(p.astype(_CD))
    m_ref[...] = m_new


def _softmax_rescale(s, m_ref, l_ref, acc_ref, p_ref, first):
    m_new = jnp.max(s, axis=-1, keepdims=True)
    if first:
        m_new = jnp.broadcast_to(m_new, m_ref.shape)
    else:
        m_prev = m_ref[...]
        m_new = jnp.maximum(m_prev, m_new)
        alpha = jnp.exp(m_prev - m_new)
        acc_ref[...] = _lanes(alpha, acc_ref.shape[-1]) * acc_ref[...]
    p = jnp.exp(s - _lanes(m_new, s.shape[1]))
    if l_ref is not None:
        l_ref[...] = _fold_lanes(p) if first else alpha * l_ref[...] + _fold_lanes(p)
    p_ref[...] = p.astype(p_ref.dtype)
    m_ref[...] = m_new


def _mla_prompt_body(qn_ref, qr_ref, ckv_ref, kr_ref, wuk_ref, wuv_ref, o_ref, ql_s, qr_s, s_s, p_s, m_s, l_s, acc_s, *, TQ):
    i = pl.program_id(1)
    rows = [slice(h * TQ, (h + 1) * TQ) for h in range(MLA_HEADS)]
    for h in range(MLA_HEADS):
        ql_s[rows[h], :] = _dot(qn_ref[:, h * MLA_NOPE:(h + 1) * MLA_NOPE], wuk_ref[h]).astype(_CD)
        qr_s[rows[h], :] = qr_ref[:, h * MLA_ROPE:(h + 1) * MLA_ROPE]
    def block(off, visible, first):
        kc = ckv_ref[pl.ds(off, TQ), :]
        s_s[...] = (_dot_nt(ql_s[...], kc) + _dot_nt(qr_s[...], kr_ref[pl.ds(off, TQ), :])) * MLA_SCALE
        for r in rows:
            s = s_s[r, :]
            if visible is not None:
                s = jnp.where(visible, s, _NEG_INF)
            _softmax_rescale(s, m_s.at[r], l_s.at[r], acc_s.at[r], p_s.at[r], first)
        pv = _dot(p_s[...], kc)
        acc_s[...] = pv if first else acc_s[...] + pv

    tri = lax.broadcasted_iota(jnp.int32, (TQ, TQ), 1) <= lax.broadcasted_iota(jnp.int32, (TQ, TQ), 0)
    block(pl.multiple_of(i * TQ, TQ), tri, True)

    def body(j, c):
        block(pl.multiple_of(j * TQ, TQ), None, False)
        return c

    lax.fori_loop(0, i, body, 0)
    for h in range(MLA_HEADS):
        o_lat = (acc_s[rows[h], :] / jnp.sum(l_s[rows[h], :], axis=-1, keepdims=True)).astype(_CD)
        o_ref[:, h * MLA_V:(h + 1) * MLA_V] = _dot(o_lat, wuv_ref[h]).astype(o_ref.dtype)


def _mla_prompt(qn, qr, ckvb, krb, wuk, wuv, B, T):
    tq = _tile(T, ATTN_TILE)
    nq = T // tq
    R = MLA_HEADS * tq
    qmap = lambda b, i: (b * nq + i, 0)
    smap = lambda b, i: (b, 0)
    c3 = lambda b, i: (0, 0, 0)
    return pl.pallas_call(
        functools.partial(_mla_prompt_body, TQ=tq), grid=(B, nq),
        in_specs=[pl.BlockSpec((tq, qn.shape[1]), qmap), pl.BlockSpec((tq, qr.shape[1]), qmap),
                  pl.BlockSpec((T, MLA_KV_LORA), smap), pl.BlockSpec((T, MLA_ROPE), smap),
                  pl.BlockSpec(wuk.shape, c3), pl.BlockSpec(wuv.shape, c3)],
        out_specs=pl.BlockSpec((tq, MLA_HEADS * MLA_V), qmap),
        out_shape=jax.ShapeDtypeStruct((B * T, MLA_HEADS * MLA_V), _CD),
        scratch_shapes=[pltpu.VMEM((R, MLA_KV_LORA), _CD), pltpu.VMEM((R, MLA_ROPE), _CD),
                        pltpu.VMEM((R, tq), _F32), pltpu.VMEM((R, tq), _CD),
                        pltpu.VMEM((R, LANES), _F32), pltpu.VMEM((R, LANES), _F32), pltpu.VMEM((R, MLA_KV_LORA), _F32)],
        compiler_params=_cparams(2), name="mla_prompt",
    )(qn, qr, ckvb, krb, wuk, wuv)


def _mla_sample_body(pt_ref, qn_ref, qr_ref, nckv_ref, nkr_ref, wuk_ref, wuv_ref, *rest, PG, Td, TdP):
    ckv_pages, krt_pages = rest[:PG], rest[PG:2 * PG]
    o_ref, ql_s, qr_s, m_ref, l_ref, acc_ref = rest[2 * PG:]
    i = pl.program_id(1)
    R = MLA_HEADS * TdP

    @pl.when(i == 0)
    def _():
        for h in range(MLA_HEADS):
            ql_s[h * TdP:(h + 1) * TdP, :] = _dot(qn_ref[0][:, h * MLA_NOPE:(h + 1) * MLA_NOPE], wuk_ref[h])
            qr_s[h * TdP:(h + 1) * TdP, :] = qr_ref[0][:, h * MLA_ROPE:(h + 1) * MLA_ROPE].astype(_F32)
        m_ref[...] = jnp.full((R, LANES), _NEG_INF, _F32)
        l_ref[...] = jnp.zeros((R, LANES), _F32)
        acc_ref[...] = jnp.zeros((R, MLA_KV_LORA), _F32)

    ql = ql_s[...].astype(_CD)
    qr = qr_s[...].astype(_CD)
    kcs = [ckv_pages[j][...].astype(_CD) for j in range(PG)]
    s = jnp.concatenate([_dot_nt(ql, kcs[j]) + _dot(qr, krt_pages[j][...].astype(_CD)) for j in range(PG)], axis=1)

    def pv(p):
        out = _dot(p[:, :PAGE_SIZE], kcs[0])
        for j in range(1, PG):
            out = out + _dot(p[:, j * PAGE_SIZE:(j + 1) * PAGE_SIZE], kcs[j])
        return out

    _softmax_step(s * MLA_SCALE, pv, m_ref, l_ref, acc_ref)

    @pl.when(i == pl.num_programs(1) - 1)
    def _():
        kc = nckv_ref[0]
        s = (_dot_nt(ql, kc) + _dot_nt(qr, nkr_ref[0])) * MLA_SCALE
        t = lax.broadcasted_iota(jnp.int32, (R, PAGE_SIZE), 0) % TdP
        c = lax.broadcasted_iota(jnp.int32, (R, PAGE_SIZE), 1)
        s = jnp.where((c <= t) & (c < Td), s, _NEG_INF)
        _softmax_step(s, lambda p: _dot(p, kc), m_ref, l_ref, acc_ref)
        o_lat = (acc_ref[...] / jnp.sum(l_ref[...], axis=-1, keepdims=True)).astype(_CD)
        for h in range(MLA_HEADS):
            o_ref[0, :, h * MLA_V:(h + 1) * MLA_V] = _dot(o_lat[h * TdP:(h + 1) * TdP], wuv_ref[h])


def _page_spec(block, layer, PG, j):
    nd = len(block)

    def index_map(b, i, pt):
        return (layer, pt[b, i * PG + j]) + (0,) * (nd - 2)

    return pl.BlockSpec(block, index_map)


def _mla_sample(page_table, qn, qr, nckv, nkr, wuk, wuv, cache_ckv, cache_krt, layer, Td):
    Bd, TdP, _ = qn.shape
    n_pages = page_table.shape[1]
    pg = _tile(n_pages, PAGES_PER_STEP)
    seq = lambda b, i, pt: (b, 0, 0)
    c3 = lambda b, i, pt: (0, 0, 0)
    R = MLA_HEADS * TdP
    in_specs = [pl.BlockSpec((1, TdP, qn.shape[2]), seq), pl.BlockSpec((1, TdP, qr.shape[2]), seq),
                pl.BlockSpec((1, PAGE_SIZE, MLA_KV_LORA), seq), pl.BlockSpec((1, PAGE_SIZE, MLA_ROPE), seq),
                pl.BlockSpec(wuk.shape, c3), pl.BlockSpec(wuv.shape, c3)]
    in_specs += [_page_spec((None, None, PAGE_SIZE, MLA_KV_LORA), layer, pg, j) for j in range(pg)]
    in_specs += [_page_spec((None, None, MLA_ROPE, PAGE_SIZE), layer, pg, j) for j in range(pg)]
    return pl.pallas_call(
        functools.partial(_mla_sample_body, PG=pg, Td=Td, TdP=TdP),
        grid_spec=pltpu.PrefetchScalarGridSpec(
            num_scalar_prefetch=1, grid=(Bd, n_pages // pg), in_specs=in_specs,
            out_specs=pl.BlockSpec((1, TdP, MLA_HEADS * MLA_V), seq),
            scratch_shapes=[pltpu.VMEM((R, MLA_KV_LORA), _F32), pltpu.VMEM((R, MLA_ROPE), _F32),
                            pltpu.VMEM((R, LANES), _F32), pltpu.VMEM((R, LANES), _F32), pltpu.VMEM((R, MLA_KV_LORA), _F32)]),
        out_shape=jax.ShapeDtypeStruct((Bd, TdP, MLA_HEADS * MLA_V), _F32),
        compiler_params=_cparams(2), name="mla_sample",
    )(page_table, qn, qr, nckv, nkr, wuk, wuv, *([cache_ckv] * pg), *([cache_krt] * pg))


_O_Q, _O_K, _O_V, _O_QI, _O_KI, _O_WI, _O_END = 0, 1024, 1280, 1536, 2048, 2176, 2304
_KV = DSA_KV_HEADS * DSA_HEAD_DIM
assert all(math.frexp(s)[0] == 0.5 for s in (DSA_SCALE, IDX_SCALE))
_V1 = LANES


def _odd_proj_body(x_ref, g_ref, w_ref, q_ref, k_ref, v_ref, qi_ref, ki_ref, wi_ref, kb_ref, vb_ref, kib_ref, v1_ref, *, time_minor):
    h = _rms(x_ref[...], g_ref[...]).astype(_CD)
    z = _dot(h, w_ref[...])
    q_ref[...] = (z[:, _O_Q:_O_K] * DSA_SCALE).astype(_CD)
    k = z[:, _O_K:_O_V]
    v = z[:, _O_V:_O_QI]
    ki = z[:, _O_KI:_O_KI + IDX_DIM]
    if time_minor:
        k_ref[0], v_ref[0], ki_ref[0] = k.T, v.T, ki.T
    else:
        k_ref[...], v_ref[...], ki_ref[...] = k, v, ki
    qi_ref[...] = (z[:, _O_QI:_O_KI] * IDX_SCALE).astype(_CD)
    wi_ref[...] = z[:, _O_WI:_O_WI + IDX_HEADS] * (IDX_HEADS ** -0.5)
    kb_ref[...] = k.astype(_CD)
    vb_ref[...] = v.astype(_CD)
    kib_ref[...] = ki.astype(_CD)
    ones = jnp.ones((v.shape[0], _V1 - DSA_HEAD_DIM), _CD)
    for n in range(DSA_KV_HEADS):
        v1_ref[:, n * _V1:n * _V1 + DSA_HEAD_DIM] = v[:, n * DSA_HEAD_DIM:(n + 1) * DSA_HEAD_DIM].astype(_CD)
        v1_ref[:, n * _V1 + DSA_HEAD_DIM:(n + 1) * _V1] = ones


def _odd_proj(x, g, w_all, seq_len=None):
    n, d = x.shape
    tm = _tile(n, TOKEN_TILE)
    const = lambda i: (0, 0)
    row = lambda i: (i, 0)
    widths = (DSA_HEADS * DSA_HEAD_DIM, _KV, _KV, IDX_HEADS * IDX_DIM, IDX_DIM, IDX_HEADS, _KV, _KV, IDX_DIM,
              DSA_KV_HEADS * _V1)
    dtypes = (_CD, _F32, _F32, _CD, _F32, _F32, _CD, _CD, _CD, _CD)
    out_specs = [pl.BlockSpec((tm, w), row) for w in widths]
    out_shape = [jax.ShapeDtypeStruct((n, w), dt) for w, dt in zip(widths, dtypes)]
    if seq_len is not None:
        for o in (1, 2, 4):
            out_specs[o], out_shape[o] = _time_minor_out(n, tm, seq_len, widths[o])
    return pl.pallas_call(
        functools.partial(_odd_proj_body, time_minor=seq_len is not None), grid=(n // tm,),
        in_specs=[pl.BlockSpec((tm, d), row), pl.BlockSpec((1, d), const), pl.BlockSpec(w_all.shape, const)],
        out_specs=out_specs, out_shape=out_shape,
        compiler_params=_cparams(1), name="odd_proj",
    )(x, g.reshape(1, d), w_all)


def _kth_largest_search(count, topk, idx_bits, shape):
    def value_bit(b, t_u):
        cand_u = t_u | lax.shift_left(jnp.int32(1), 31 - b)
        cand = cand_u ^ _MIN_I32
        return jnp.where(count(lambda k, idx: k >= cand) >= topk, cand_u, t_u)

    thr = lax.fori_loop(0, 32, value_bit, jnp.zeros(shape, jnp.int32)) ^ _MIN_I32
    need = topk - count(lambda k, idx: k > thr)
    n_ge = count(lambda k, idx: k >= thr)

    def index_bit(b, cut):
        cand = cut | lax.shift_left(jnp.int32(1), idx_bits - 1 - b)
        return jnp.where(count(lambda k, idx: (k == thr) & (idx < cand)) < need, cand, cut)

    cut = lax.cond(jnp.max(n_ge) > topk,
                   lambda: lax.fori_loop(0, idx_bits, index_bit, jnp.zeros(shape, jnp.int32)),
                   lambda: jnp.full(shape, (1 << idx_bits) - 1, jnp.int32))
    return thr, cut


def _selected(kk, idx, thr, cut):
    return (kk > thr) | ((kk == thr) & (idx <= cut))


def _dsa_prompt_body(q_ref, qi_ref, wi_ref, k_ref, v1_ref, ki_ref, o_ref, key_s, bias_s, wib_s, qs, s_s, p_s, m_s, acc_s,
                     *, TQ, T, topk):
    i = pl.program_id(1)
    rowpos = i * TQ + lax.broadcasted_iota(jnp.int32, (TQ, 1), 0)
    col0 = lax.broadcasted_iota(jnp.int32, (1, TQ), 1)
    for h in range(IDX_HEADS):
        wib_s[h] = jnp.broadcast_to(wi_ref[:, h:h + 1], (TQ, LANES))

    def select(nch):
        blocks = [slice(j * TQ, (j + 1) * TQ) for j in range(nch)]
        if nch * TQ <= topk:
            assert nch == 1
            bias_s[:, blocks[0]] = jnp.where(col0 <= rowpos, 0.0, _MASKED)
            return
        for j, js in enumerate(blocks):
            kic = ki_ref[js, :]
            isc = jnp.zeros((TQ, TQ), _F32)
            for h in range(IDX_HEADS):
                s = _dot_nt(qi_ref[:, h * IDX_DIM:(h + 1) * IDX_DIM], kic)
                isc = isc + jnp.maximum(s, 0.0) * _lanes(wib_s[h], TQ)
            if j == nch - 1:
                isc = jnp.where(j * TQ + col0 <= rowpos, isc, _NEG_INF)
            key_s[:, js] = _sort_key(isc)

        def count(pred):
            acc = None
            for j, js in enumerate(blocks):
                part = _fold_lanes(jnp.where(pred(key_s[:, js], j * TQ + col0), 1.0, 0.0))
                acc = part if acc is None else acc + part
            return jnp.sum(acc, axis=1, keepdims=True)

        thr, cut = _kth_largest_search(count, float(topk), max(1, (T - 1).bit_length()), (TQ, 1))
        for j, js in enumerate(blocks):
            sel = _selected(key_s[:, js], j * TQ + col0, thr, cut)
            if j == nch - 1:
                sel = sel & (j * TQ + col0 <= rowpos)
            bias_s[:, js] = jnp.where(sel, 0.0, _MASKED)

    for c in range(T // TQ):
        pl.when(i == c)(functools.partial(select, c + 1))

    G = DSA_GROUP * TQ
    for hh in range(DSA_HEADS):
        qs[hh * TQ:(hh + 1) * TQ, :] = q_ref[:, hh * DSA_HEAD_DIM:(hh + 1) * DSA_HEAD_DIM]

    def block(off, first):
        def kv_head(n):
            g = slice(n * G, (n + 1) * G)
            s_s[g, :] = _dot_nt(qs[g, :], k_ref[pl.ds(off, TQ), n * DSA_HEAD_DIM:(n + 1) * DSA_HEAD_DIM])
            for hh in range(n * DSA_GROUP, (n + 1) * DSA_GROUP):
                r = slice(hh * TQ, (hh + 1) * TQ)
                _softmax_rescale(s_s[r, :] + bias_s[:, pl.ds(off, TQ)], m_s.at[r], None, acc_s.at[r], p_s.at[r], first)
            pv = _dot(p_s[g, :], v1_ref[pl.ds(off, TQ), n * _V1:(n + 1) * _V1])
            acc_s[g, :] = pv if first else acc_s[g, :] + pv

        for n in range(DSA_KV_HEADS):
            pl.when(i >= 0)(functools.partial(kv_head, n))

    block(0, True)

    def body(j, c):
        block(pl.multiple_of(j * TQ, TQ), False)
        return c

    lax.fori_loop(1, i + 1, body, 0)
    for hh in range(DSA_HEADS):
        a = acc_s[hh * TQ:(hh + 1) * TQ, :]
        o_ref[:, hh * DSA_HEAD_DIM:(hh + 1) * DSA_HEAD_DIM] = (a[:, :DSA_HEAD_DIM] / a[:, _V1 - DSA_HEAD_DIM:]).astype(o_ref.dtype)


def _dsa_prompt(q, qi, wi, kb, v1, kib, B, T):
    tq = _tile(T, ATTN_TILE)
    nq = T // tq
    topk = min(TOPK_MAX, T // 4)
    qmap = lambda b, i: (b * nq + i, 0)
    smap = lambda b, i: (b, 0)
    return pl.pallas_call(
        functools.partial(_dsa_prompt_body, TQ=tq, T=T, topk=topk), grid=(B, nq),
        in_specs=[pl.BlockSpec((tq, q.shape[1]), qmap), pl.BlockSpec((tq, qi.shape[1]), qmap),
                  pl.BlockSpec((tq, IDX_HEADS), qmap),
                  pl.BlockSpec((T, _KV), smap), pl.BlockSpec((T, v1.shape[1]), smap), pl.BlockSpec((T, IDX_DIM), smap)],
        out_specs=pl.BlockSpec((tq, DSA_HEADS * DSA_HEAD_DIM), qmap),
        out_shape=jax.ShapeDtypeStruct((B * T, DSA_HEADS * DSA_HEAD_DIM), _CD),
        scratch_shapes=[pltpu.VMEM((tq, T), jnp.int32), pltpu.VMEM((tq, T), _F32),
                        pltpu.VMEM((IDX_HEADS, tq, LANES), _F32), pltpu.VMEM((DSA_HEADS * tq, DSA_HEAD_DIM), _CD),
                        pltpu.VMEM((DSA_HEADS * tq, tq), _F32), pltpu.VMEM((DSA_HEADS * tq, tq), _CD),
                        pltpu.VMEM((DSA_HEADS * tq, LANES), _F32), pltpu.VMEM((DSA_HEADS * tq, _V1), _F32)],
        compiler_params=_cparams(2), name="dsa_prompt",
    )(q, qi, wi, kb, v1, kib)


def _dsa_keys_body(pt_ref, qi_ref, wi_ref, nki_ref, *rest, PG, P, Td, TdP):
    kit_pages = rest[:PG]
    key_ref, qs, ws = rest[PG:]
    i = pl.program_id(1)

    @pl.when(i == 0)
    def _():
        for h in range(IDX_HEADS):
            qs[h * TdP:(h + 1) * TdP, :] = qi_ref[0][:, h * IDX_DIM:(h + 1) * IDX_DIM].astype(_F32)
            ws[h * TdP:(h + 1) * TdP, :] = jnp.broadcast_to(wi_ref[0][:, h:h + 1], (TdP, PAGE_SIZE))

    def index_scores(qk):
        s = jnp.maximum(qk, 0.0) * ws[...]
        isc = s[0:TdP]
        for h in range(1, IDX_HEADS):
            isc = isc + s[h * TdP:(h + 1) * TdP]
        return isc

    qb = qs[...].astype(_CD)
    for j in range(PG):
        off = pl.multiple_of((i * PG + j) * PAGE_SIZE, PAGE_SIZE)
        key_ref[0, :, pl.ds(off, PAGE_SIZE)] = _sort_key(index_scores(_dot(qb, kit_pages[j][...].astype(_CD))))

    @pl.when(i == pl.num_programs(1) - 1)
    def _():
        isc = index_scores(_dot_nt(qb, nki_ref[0]))
        t = lax.broadcasted_iota(jnp.int32, (TdP, PAGE_SIZE), 0)
        c = lax.broadcasted_iota(jnp.int32, (TdP, PAGE_SIZE), 1)
        key_ref[0, :, P:P + PAGE_SIZE] = _sort_key(jnp.where((c <= t) & (c < Td), isc, _NEG_INF))


def _dsa_keys(page_table, qi, wi, nki, cache_kit, layer, Td):
    Bd, TdP, _ = qi.shape
    n_pages = page_table.shape[1]
    P = n_pages * PAGE_SIZE
    pg = _tile(n_pages, PAGES_PER_STEP)
    seq = lambda b, i, pt: (b, 0, 0)
    R = IDX_HEADS * TdP
    in_specs = [pl.BlockSpec((1, TdP, qi.shape[2]), seq), pl.BlockSpec((1, TdP, IDX_HEADS), seq),
                pl.BlockSpec((1, PAGE_SIZE, IDX_DIM), seq)]
    in_specs += [_page_spec((None, None, IDX_DIM, PAGE_SIZE), layer, pg, j) for j in range(pg)]
    return pl.pallas_call(
        functools.partial(_dsa_keys_body, PG=pg, P=P, Td=Td, TdP=TdP),
        grid_spec=pltpu.PrefetchScalarGridSpec(
            num_scalar_prefetch=1, grid=(Bd, n_pages // pg), in_specs=in_specs,
            out_specs=pl.BlockSpec((1, TdP, P + PAGE_SIZE), seq),
            scratch_shapes=[pltpu.VMEM((R, IDX_DIM), _F32), pltpu.VMEM((R, PAGE_SIZE), _F32)]),
        out_shape=jax.ShapeDtypeStruct((Bd, TdP, P + PAGE_SIZE), jnp.int32),
        compiler_params=_cparams(2), name="dsa_keys",
    )(page_table, qi, wi, nki, *([cache_kit] * pg))


def _dsa_threshold_body(key_ref, thr_ref, cut_ref, *, topk):
    rows, width = key_ref.shape
    idx = lax.broadcasted_iota(jnp.int32, (1, width), 1)

    def count(pred):
        hit = jnp.where(pred(key_ref[...], idx), 1.0, 0.0)
        parts = [hit[:, c * LANES:(c + 1) * LANES] for c in range(width // LANES)]
        while len(parts) > 1:
            parts = [a + b for a, b in zip(parts[::2], parts[1::2])] + ([parts[-1]] if len(parts) % 2 else [])
        return jnp.sum(parts[0], axis=1, keepdims=True)

    thr, cut = _kth_largest_search(count, float(topk), (width - 1).bit_length(), (rows, 1))
    thr_ref[...] = jnp.broadcast_to(thr, (rows, LANES))
    cut_ref[...] = jnp.broadcast_to(cut, (rows, LANES))


def _dsa_threshold(keys, topk):
    rows, width = keys.shape
    rb = _tile(rows, SEARCH_ROWS)
    row = lambda i: (i, 0)
    return pl.pallas_call(
        functools.partial(_dsa_threshold_body, topk=topk), grid=(rows // rb,),
        in_specs=[pl.BlockSpec((rb, width), row)],
        out_specs=[pl.BlockSpec((rb, LANES), row), pl.BlockSpec((rb, LANES), row)],
        out_shape=[jax.ShapeDtypeStruct((rows, LANES), jnp.int32)] * 2,
        compiler_params=_cparams(1), name="dsa_threshold",
    )(keys)


def _dsa_sample_body(pt_ref, q_ref, key_ref, nkey_ref, thr_ref, cut_ref, nk_ref, nv_ref, *rest, PG, P, Td, TdP):
    kt_pages, vt_pages = rest[:PG], rest[PG:2 * PG]
    o_ref, qs, m_ref, l_ref, acc_ref = rest[2 * PG:]
    i = pl.program_id(1)
    RH = DSA_GROUP * TdP
    R = DSA_KV_HEADS * RH
    heads = [slice(n * RH, (n + 1) * RH) for n in range(DSA_KV_HEADS)]

    @pl.when(i == 0)
    def _():
        for hh in range(DSA_HEADS):
            qs[hh * TdP:(hh + 1) * TdP, :] = q_ref[0][:, hh * DSA_HEAD_DIM:(hh + 1) * DSA_HEAD_DIM].astype(_F32)
        m_ref[...] = jnp.full((R, LANES), _MASKED, _F32)
        l_ref[...] = jnp.zeros((R, LANES), _F32)
        acc_ref[...] = jnp.zeros((R, DSA_HEAD_DIM), _F32)

    thr = thr_ref[0][:, 0:1]
    cut = cut_ref[0][:, 0:1]
    qb = qs[...].astype(_CD)

    def attend(kk, idx, visible, scores, pv):
        sel = _selected(kk, idx, thr, cut)
        if visible is not None:
            sel = sel & visible
        bias = jnp.concatenate([jnp.where(sel, 0.0, _MASKED)] * (R // TdP), axis=0)
        s = jnp.concatenate([scores(n, qb[heads[n]]) for n in range(DSA_KV_HEADS)], axis=0) + bias
        _softmax_step(s, lambda p: jnp.concatenate([pv(n, p[heads[n]]) for n in range(DSA_KV_HEADS)], axis=0),
                      m_ref, l_ref, acc_ref)

    kts = [kt_pages[j][...].astype(_CD) for j in range(PG)]
    vts = [vt_pages[j][...].astype(_CD) for j in range(PG)]
    W = PG * PAGE_SIZE
    idx = i * W + lax.broadcasted_iota(jnp.int32, (TdP, W), 1)

    def past_pv(n, p):
        out = _dot_nt(p[:, :PAGE_SIZE], vts[0][n])
        for j in range(1, PG):
            out = out + _dot_nt(p[:, j * PAGE_SIZE:(j + 1) * PAGE_SIZE], vts[j][n])
        return out

    attend(key_ref[0], idx, None,
           lambda n, q: jnp.concatenate([_dot(q, kts[j][n]) for j in range(PG)], axis=1), past_pv)

    @pl.when(i == pl.num_programs(1) - 1)
    def _():
        t = lax.broadcasted_iota(jnp.int32, (TdP, PAGE_SIZE), 0)
        c = lax.broadcasted_iota(jnp.int32, (TdP, PAGE_SIZE), 1)
        attend(nkey_ref[0], P + c, (c <= t) & (c < Td),
               lambda n, q: _dot_nt(q, nk_ref[0][:, n * DSA_HEAD_DIM:(n + 1) * DSA_HEAD_DIM]),
               lambda n, p: _dot(p, nv_ref[0][:, n * DSA_HEAD_DIM:(n + 1) * DSA_HEAD_DIM]))
        o = acc_ref[...] / jnp.sum(l_ref[...], axis=-1, keepdims=True)
        for hh in range(DSA_HEADS):
            o_ref[0, :, hh * DSA_HEAD_DIM:(hh + 1) * DSA_HEAD_DIM] = o[hh * TdP:(hh + 1) * TdP]


def _dsa_sample(page_table, q, keys, nkeys, thr, cut, nk, nv, cache_kt, cache_vt, layer, Td):
    Bd, TdP, _ = q.shape
    n_pages = page_table.shape[1]
    P = n_pages * PAGE_SIZE
    pg = _tile(n_pages, PAGES_PER_STEP)
    seq = lambda b, i, pt: (b, 0, 0)
    R = DSA_HEADS * TdP
    in_specs = [pl.BlockSpec((1, TdP, q.shape[2]), seq),
                pl.BlockSpec((1, TdP, pg * PAGE_SIZE), lambda b, i, pt: (b, 0, i)),
                pl.BlockSpec((1, TdP, PAGE_SIZE), seq), pl.BlockSpec((1, TdP, LANES), seq),
                pl.BlockSpec((1, TdP, LANES), seq),
                pl.BlockSpec((1, PAGE_SIZE, _KV), seq), pl.BlockSpec((1, PAGE_SIZE, _KV), seq)]
    page_block = (None, None, DSA_KV_HEADS, DSA_HEAD_DIM, PAGE_SIZE)
    in_specs += [_page_spec(page_block, layer, pg, j) for j in range(pg)]
    in_specs += [_page_spec(page_block, layer, pg, j) for j in range(pg)]
    return pl.pallas_call(
        functools.partial(_dsa_sample_body, PG=pg, P=P, Td=Td, TdP=TdP),
        grid_spec=pltpu.PrefetchScalarGridSpec(
            num_scalar_prefetch=1, grid=(Bd, n_pages // pg), in_specs=in_specs,
            out_specs=pl.BlockSpec((1, TdP, DSA_HEADS * DSA_HEAD_DIM), seq),
            scratch_shapes=[pltpu.VMEM((R, DSA_HEAD_DIM), _F32), pltpu.VMEM((R, LANES), _F32),
                            pltpu.VMEM((R, LANES), _F32), pltpu.VMEM((R, DSA_HEAD_DIM), _F32)]),
        out_shape=jax.ShapeDtypeStruct((Bd, TdP, DSA_HEADS * DSA_HEAD_DIM), _F32),
        compiler_params=_cparams(2), name="dsa_sample",
    )(page_table, q, keys, nkeys, thr, cut, nk, nv, *([cache_kt] * pg), *([cache_vt] * pg))


def _rope_tables(pos):
    half = MLA_ROPE // 2
    inv = ROPE_THETA ** (-jnp.arange(half, dtype=_F32) / half)
    ang = pos.astype(_F32)[:, None] * inv[None, :]
    cos, sin = jnp.cos(ang), jnp.sin(ang)
    return (jnp.tile(jnp.concatenate([cos, cos], axis=1), (1, MLA_HEADS)),
            jnp.tile(jnp.concatenate([-sin, sin], axis=1), (1, MLA_HEADS)))


def _pack_cols(w, pieces, total):
    out = jnp.zeros((w.shape[0], total), w.dtype)
    for start, cols in pieces:
        out = out.at[:, start:start + len(cols)].set(w[:, np.asarray(cols)])
    return out.astype(_CD)


def _swap_halves(n):
    return (np.arange(n) + n // 2) % n


def _pad_rows(a, rows):
    return jnp.pad(a, ((0, 0), (0, rows - a.shape[1]), (0, 0)))


def kernel(x_prompt, x_sample, cache_mla_ckv, cache_mla_kr, cache_dsa_k, cache_dsa_v, cache_dsa_kidx, state_pool, page_table, ffn_norm, ffn_w1, ffn_w3, ffn_w2, mix_norm, final_norm, even_w_in, mla_q_norm, mla_kv_norm, mla_w_uq, mla_w_uk, mla_w_uv, pool_w, pool_scale, even_w_out, odd_w_in, odd_w_out):
    B, T, D = x_prompt.shape
    Bd, Td, _ = x_sample.shape
    depth = ffn_norm.shape[0]
    P = page_table.shape[1] * PAGE_SIZE
    TdP = -(-Td // SUBLANES) * SUBLANES
    assert D == D_MODEL and Td <= PAGE_SIZE and T >= POOL_STATE
    Np, Ns = B * T, Bd * Td
    tm_s = _tile(Ns, TOKEN_TILE)
    assert tm_s % Td == 0

    pos_p = jnp.arange(T, dtype=jnp.int32)
    pos_s = P + jnp.arange(Td, dtype=jnp.int32)
    cos_p, sin_p = _rope_tables(pos_p)
    cos_s, sin_s = (jnp.tile(a, (tm_s // Td, 1)) for a in _rope_tables(pos_s))

    def seq_rows(a, rows=TdP):
        return _pad_rows(a.reshape(Bd, Td, a.shape[1]), rows)

    cache_krt = jnp.swapaxes(cache_mla_kr, 2, 3)
    cache_kit = jnp.swapaxes(cache_dsa_kidx, 2, 3)
    cache_kt = jnp.transpose(cache_dsa_k, (0, 1, 3, 4, 2))
    cache_vt = jnp.transpose(cache_dsa_v, (0, 1, 3, 4, 2))

    xp = x_prompt.reshape(Np, D)
    xs = x_sample.reshape(Ns, D)
    pre_p, pre_s = (), ()
    outs = {k: [] for k in ("p_ckv", "p_kr", "p_k", "p_v", "p_ki", "p_pool", "s_ckv", "s_kr", "s_k", "s_v", "s_ki", "s_pool")}
    ffn_w = [w.astype(_CD) for w in (ffn_w1, ffn_w3, ffn_w2)]
    for l in range(depth):
        j = l // 2
        wa = [w[l, 0] for w in ffn_w]
        wb = [w[l, 1] for w in ffn_w]
        xp = _ffn(xp, ffn_norm[l, 0], *wa, pre=pre_p)
        xs = _ffn(xs, ffn_norm[l, 0], *wa, pre=pre_s)
        if l % 2 == 0:
            kr0 = POOL_DIM + MLA_Q_LORA + MLA_KV_LORA
            w_all = _pack_cols(even_w_in[j], [(_E_U, np.arange(kr0)), (_E_KR, kr0 + np.arange(MLA_ROPE)),
                                              (_E_KRS, kr0 + _swap_halves(MLA_ROPE))], _E_END)
            per_head = MLA_NOPE + MLA_ROPE
            heads = np.arange(MLA_HEADS)[:, None] * per_head
            wuq_all = _pack_cols(mla_w_uq[j], [(_Q_NOPE, (heads + np.arange(MLA_NOPE)[None]).ravel()),
                                               (_Q_ROPE, (heads + MLA_NOPE + np.arange(MLA_ROPE)[None]).ravel()),
                                               (_Q_ROPES, (heads + MLA_NOPE + _swap_halves(MLA_ROPE)[None]).ravel())], _Q_END)
            wuk = jnp.transpose(mla_w_uk[j], (1, 2, 0)).astype(_CD)
            wuv = jnp.transpose(mla_w_uv[j], (1, 0, 2)).astype(_CD)
            w_out = even_w_out[j].astype(_CD)
            pw = pool_w[j].astype(_CD)
            u, qn, qr, ckv, kr, ckvb, krb = _even_proj(xp, mix_norm[l], w_all, mla_q_norm[j], mla_kv_norm[j], wuq_all, cos_p, sin_p, T)
            y_pool = _pool_prompt(u, pw, pool_scale[j], B, T)
            o = _mla_prompt(qn, qr, ckvb, krb, wuk, wuv, B, T)
            pre_p = ((y_pool, w_out[:POOL_DIM]), (o, w_out[POOL_DIM:]))
            outs["p_ckv"].append(ckv.reshape(B, T, MLA_KV_LORA))
            outs["p_kr"].append(jnp.swapaxes(kr, 1, 2))
            outs["p_pool"].append(u.reshape(B, T, POOL_DIM)[:, T - POOL_STATE:])
            u, qn, qr, ckv, kr, ckvb, krb = _even_proj(xs, mix_norm[l], w_all, mla_q_norm[j], mla_kv_norm[j], wuq_all, cos_s, sin_s)
            ext = jnp.concatenate([state_pool[j], u.reshape(Bd, Td, POOL_DIM)], axis=1)
            y_pool = _pool_sample(jnp.swapaxes(ext, 0, 1), pw, pool_scale[j], Td, P)
            y_pool = jnp.swapaxes(y_pool, 0, 1).reshape(Ns, POOL_DIM)
            o = _mla_sample(page_table, seq_rows(qn), seq_rows(qr), seq_rows(ckvb, PAGE_SIZE), seq_rows(krb, PAGE_SIZE),
                            wuk, wuv, cache_mla_ckv, cache_krt, j, Td)
            o = o[:, :Td].reshape(Ns, MLA_HEADS * MLA_V).astype(_CD)
            pre_s = ((y_pool, w_out[:POOL_DIM]), (o, w_out[POOL_DIM:]))
            outs["s_ckv"].append(ckv.reshape(Bd, Td, MLA_KV_LORA))
            outs["s_kr"].append(kr.reshape(Bd, Td, MLA_ROPE))
            outs["s_pool"].append(ext[:, -POOL_STATE:])
        else:
            o5 = (DSA_HEADS + 2 * DSA_KV_HEADS) * DSA_HEAD_DIM + IDX_HEADS * IDX_DIM + IDX_DIM
            o4 = o5 - IDX_DIM
            w_all = _pack_cols(odd_w_in[j], [(_O_Q, np.arange(o4)), (_O_KI, o4 + np.arange(IDX_DIM)),
                                             (_O_WI, o5 + np.arange(IDX_HEADS))], _O_END)
            w_out = odd_w_out[j].astype(_CD)
            q, k, v, qi, ki, wi, kb, vb, kib, v1 = _odd_proj(xp, mix_norm[l], w_all, T)
            o = _dsa_prompt(q, qi, wi, kb, v1, kib, B, T)
            pre_p = ((o, w_out),)
            per_head = lambda a: jnp.transpose(a.reshape(B, DSA_KV_HEADS, DSA_HEAD_DIM, T), (0, 3, 1, 2))
            outs["p_k"].append(per_head(k))
            outs["p_v"].append(per_head(v))
            outs["p_ki"].append(jnp.swapaxes(ki, 1, 2))
            q, k, v, qi, ki, wi, kb, vb, kib, _ = _odd_proj(xs, mix_norm[l], w_all)
            keys = _dsa_keys(page_table, seq_rows(qi), seq_rows(wi), seq_rows(kib, PAGE_SIZE), cache_kit, j, Td)
            thr, cut = _dsa_threshold(keys[:, :Td].reshape(Ns, P + PAGE_SIZE), min(TOPK_MAX, (P + Td) // 4))
            o = _dsa_sample(page_table, seq_rows(q), keys, keys[:, :, P:], seq_rows(thr), seq_rows(cut), seq_rows(kb, PAGE_SIZE),
                            seq_rows(vb, PAGE_SIZE), cache_kt, cache_vt, j, Td)
            o = o[:, :Td].reshape(Ns, DSA_HEADS * DSA_HEAD_DIM).astype(_CD)
            pre_s = ((o, w_out),)
            outs["s_k"].append(k.reshape(Bd, Td, DSA_KV_HEADS, DSA_HEAD_DIM))
            outs["s_v"].append(v.reshape(Bd, Td, DSA_KV_HEADS, DSA_HEAD_DIM))
            outs["s_ki"].append(ki.reshape(Bd, Td, IDX_DIM))
        last = l == depth - 1
        xp = _ffn(xp, ffn_norm[l, 1], *wb, pre=pre_p, final_g=final_norm if last else None)
        xs = _ffn(xs, ffn_norm[l, 1], *wb, pre=pre_s, final_g=final_norm if last else None)
        pre_p, pre_s = (), ()
    st = {k: jnp.stack(v) for k, v in outs.items()}
    return (xp.reshape(B, T, D), xs.reshape(Bd, Td, D),
            st["p_ckv"], st["p_kr"], st["p_k"], st["p_v"], st["p_ki"], st["p_pool"],
            st["s_ckv"], st["s_kr"], st["s_k"], st["s_v"], st["s_ki"], st["s_pool"])
```

```python
import functools
import math

import numpy as np
import jax
import jax.numpy as jnp
from jax import lax
from jax.experimental import pallas as pl
from jax.experimental.pallas import tpu as pltpu

D_MODEL = 1024
D_FF = 2 * D_MODEL
EPS = 1e-6
PAGE_SIZE = 128
POOL_WINDOWS = (2, 4, 8, 16)
POOL_GROUPS = len(POOL_WINDOWS)
POOL_GROUP_DIM = D_MODEL // 8
POOL_DIM = POOL_GROUPS * POOL_GROUP_DIM
POOL_STATE = max(POOL_WINDOWS) - 1
MLA_HEADS = 8
MLA_NOPE = D_MODEL // 16
MLA_ROPE = D_MODEL // 32
MLA_V = D_MODEL // 16
MLA_Q_LORA = 3 * D_MODEL // 8
MLA_KV_LORA = D_MODEL // 4
MLA_SCALE = (MLA_NOPE + MLA_ROPE) ** -0.5
ROPE_THETA = 10000.0
DSA_HEADS = 16
DSA_KV_HEADS = 4
DSA_GROUP = DSA_HEADS // DSA_KV_HEADS
DSA_HEAD_DIM = D_MODEL // 16
DSA_SCALE = DSA_HEAD_DIM ** -0.5
IDX_HEADS = 8
IDX_DIM = D_MODEL // 16
IDX_SCALE = IDX_DIM ** -0.5
TOPK_MAX = 256

LANES = 128
SUBLANES = 8
VMEM_LIMIT = 56 * 1024 * 1024

TOKEN_TILE = 512
FF_CHUNK = 512
ATTN_TILE = 256
SEARCH_ROWS = 64
PAGES_PER_STEP = 64

_CD = jnp.bfloat16
_F32 = jnp.float32
_NT = (((1,), (1,)), ((), ()))
_MIN_I32 = np.int32(-2 ** 31)
_MASKED = -1e30
_NEG_INF = float("-inf")


def _cparams(n_grid):
    return pltpu.CompilerParams(dimension_semantics=("arbitrary",) * n_grid, vmem_limit_bytes=VMEM_LIMIT)


def _dot(a, b):
    return jnp.dot(a, b, preferred_element_type=_F32)


def _dot_nt(a, b):
    return lax.dot_general(a, b, _NT, preferred_element_type=_F32)


def _rms(x, g):
    return x * lax.rsqrt(jnp.mean(x * x, axis=-1, keepdims=True) + EPS) * g


def _sort_key(x):
    bits = lax.bitcast_convert_type(x, jnp.int32)
    return jnp.where(bits < 0, bits ^ np.int32(0x7FFFFFFF), bits)


def _tile(n, cap):
    t = min(cap, n)
    assert n % t == 0, (n, t)
    return t


def _ffn_body(*refs, n_pre, final):
    x_ref, g_ref, w1_ref, w3_ref, w2_ref = refs[:5]
    pre = refs[5:5 + 2 * n_pre]
    rest = refs[5 + 2 * n_pre:]
    o_ref = rest[-1]
    x = x_ref[...]
    if n_pre:
        d = _dot(pre[0][...], pre[1][...])
        for k in range(1, n_pre):
            d = d + _dot(pre[2 * k][...], pre[2 * k + 1][...])
        x = x + d
    h = _rms(x, g_ref[...]).astype(_CD)
    acc = jnp.zeros_like(x)
    for c in range(D_FF // FF_CHUNK):
        sl = slice(c * FF_CHUNK, (c + 1) * FF_CHUNK)
        a = _dot(h, w1_ref[:, sl])
        b = _dot(h, w3_ref[:, sl])
        acc = acc + _dot((a * jax.nn.sigmoid(a) * b).astype(_CD), w2_ref[sl, :])
    y = x + 0.5 * acc
    if final:
        y = _rms(y, rest[0][...])
    o_ref[...] = y


def _ffn(x, g, w1, w3, w2, pre=(), final_g=None):
    n, d = x.shape
    tm = _tile(n, TOKEN_TILE)
    const = lambda i: (0, 0)
    row = lambda i: (i, 0)
    in_specs = [pl.BlockSpec((tm, d), row), pl.BlockSpec((1, d), const),
                pl.BlockSpec(w1.shape, const), pl.BlockSpec(w3.shape, const), pl.BlockSpec(w2.shape, const)]
    args = [x, g.reshape(1, d), w1, w3, w2]
    for a, w in pre:
        in_specs += [pl.BlockSpec((tm, a.shape[1]), row), pl.BlockSpec(w.shape, const)]
        args += [a, w]
    if final_g is not None:
        in_specs.append(pl.BlockSpec((1, d), const))
        args.append(final_g.reshape(1, d))
    return pl.pallas_call(
        functools.partial(_ffn_body, n_pre=len(pre), final=final_g is not None),
        grid=(n // tm,), in_specs=in_specs, out_specs=pl.BlockSpec((tm, d), row),
        out_shape=jax.ShapeDtypeStruct((n, d), _F32), compiler_params=_cparams(1), name="ffn_half",
    )(*args)


_E_U, _E_CQ, _E_CKV, _E_KR, _E_KRS, _E_END = 0, 512, 896, 1152, 1280, 1408
_Q_NOPE, _Q_ROPE, _Q_ROPES, _Q_END = 0, 512, 768, 1024
_QR = MLA_HEADS * MLA_ROPE


def _even_proj_body(x_ref, g_ref, w_ref, gq_ref, gkv_ref, wuq_ref, cos_ref, sin_ref,
                    u_ref, qn_ref, qr_ref, ckv_ref, kr_ref, ckvb_ref, krb_ref, *, time_minor):
    h = _rms(x_ref[...], g_ref[...]).astype(_CD)
    z = _dot(h, w_ref[...])
    u_ref[...] = z[:, _E_U:_E_CQ]
    cq = _rms(z[:, _E_CQ:_E_CKV], gq_ref[...]).astype(_CD)
    ckv = _rms(z[:, _E_CKV:_E_KR], gkv_ref[...])
    cos = cos_ref[...]
    sin = sin_ref[...]
    kr = z[:, _E_KR:_E_KR + MLA_ROPE] * cos[:, :MLA_ROPE] + z[:, _E_KRS:_E_KRS + MLA_ROPE] * sin[:, :MLA_ROPE]
    q = _dot(cq, wuq_ref[...])
    qn_ref[...] = q[:, _Q_NOPE:_Q_ROPE].astype(_CD)
    qr_ref[...] = (q[:, _Q_ROPE:_Q_ROPES] * cos + q[:, _Q_ROPES:_Q_END] * sin).astype(_CD)
    ckv_ref[...] = ckv
    if time_minor:
        kr_ref[0] = kr.T
    else:
        kr_ref[...] = kr
    ckvb_ref[...] = ckv.astype(_CD)
    krb_ref[...] = kr.astype(_CD)


def _time_minor_out(n, tm, seq_len, width):
    per_seq = seq_len // tm
    return (pl.BlockSpec((1, width, tm), lambda i: (i // per_seq, 0, i % per_seq)),
            jax.ShapeDtypeStruct((n // seq_len, width, seq_len), _F32))


def _even_proj(x, g, w_all, gq, gkv, wuq_all, cos_t, sin_t, seq_len=None):
    n, d = x.shape
    tm = _tile(n, TOKEN_TILE)
    nblk = cos_t.shape[0] // tm
    const = lambda i: (0, 0)
    row = lambda i: (i, 0)
    tab = lambda i: (i % nblk, 0)
    widths = (POOL_DIM, MLA_HEADS * MLA_NOPE, _QR, MLA_KV_LORA, MLA_ROPE, MLA_KV_LORA, MLA_ROPE)
    dtypes = (_F32, _CD, _CD, _F32, _F32, _CD, _CD)
    out_specs = [pl.BlockSpec((tm, w), row) for w in widths]
    out_shape = [jax.ShapeDtypeStruct((n, w), dt) for w, dt in zip(widths, dtypes)]
    if seq_len is not None:
        out_specs[4], out_shape[4] = _time_minor_out(n, tm, seq_len, MLA_ROPE)
    return pl.pallas_call(
        functools.partial(_even_proj_body, time_minor=seq_len is not None), grid=(n // tm,),
        in_specs=[pl.BlockSpec((tm, d), row), pl.BlockSpec((1, d), const), pl.BlockSpec(w_all.shape, const),
                  pl.BlockSpec((1, MLA_Q_LORA), const), pl.BlockSpec((1, MLA_KV_LORA), const),
                  pl.BlockSpec(wuq_all.shape, const), pl.BlockSpec((tm, _QR), tab), pl.BlockSpec((tm, _QR), tab)],
        out_specs=out_specs, out_shape=out_shape,
        compiler_params=_cparams(1), name="even_proj",
    )(x, g.reshape(1, d), w_all, gq.reshape(1, -1), gkv.reshape(1, -1), wuq_all, cos_t, sin_t)


_HIST_PAD = 2 * SUBLANES


def _pool_prompt_body(u_ref, w_ref, sc_ref, y_ref, ext_ref, *, T):
    ext_ref[0:_HIST_PAD, :] = jnp.zeros((_HIST_PAD, POOL_DIM), _F32)
    ext_ref[_HIST_PAD:_HIST_PAD + T, :] = u_ref[...]
    pos = lax.broadcasted_iota(jnp.int32, (T, 1), 0)
    for g, w in enumerate(POOL_WINDOWS):
        sl = slice(g * POOL_GROUP_DIM, (g + 1) * POOL_GROUP_DIM)
        s = ext_ref[_HIST_PAD:_HIST_PAD + T, sl]
        for j in range(1, w):
            s = s + ext_ref[_HIST_PAD - j:_HIST_PAD - j + T, sl]
        cnt = jnp.minimum(pos + 1, w).astype(_F32)
        pooled = (s / cnt - u_ref[:, sl]).astype(_CD)
        y_ref[:, sl] = (_dot(pooled, w_ref[g]) * sc_ref[:, sl]).astype(y_ref.dtype)


def _pool_prompt(u, w_pool, scale, B, T):
    return pl.pallas_call(
        functools.partial(_pool_prompt_body, T=T), grid=(B,),
        in_specs=[pl.BlockSpec((T, POOL_DIM), lambda b: (b, 0)),
                  pl.BlockSpec(w_pool.shape, lambda b: (0, 0, 0)), pl.BlockSpec((1, POOL_DIM), lambda b: (0, 0))],
        out_specs=pl.BlockSpec((T, POOL_DIM), lambda b: (b, 0)),
        out_shape=jax.ShapeDtypeStruct((B * T, POOL_DIM), _CD),
        scratch_shapes=[pltpu.VMEM((_HIST_PAD + T, POOL_DIM), _F32)],
        compiler_params=_cparams(1), name="pool_prompt",
    )(u, w_pool, scale.reshape(1, POOL_DIM))


def _pool_sample_body(ext_ref, w_ref, sc_ref, y_ref, *, Td, pos0):
    for t in range(Td):
        for g, w in enumerate(POOL_WINDOWS):
            sl = slice(g * POOL_GROUP_DIM, (g + 1) * POOL_GROUP_DIM)
            s = ext_ref[POOL_STATE + t, :, sl]
            for j in range(1, w):
                s = s + ext_ref[POOL_STATE + t - j, :, sl]
            cnt = float(min(pos0 + t + 1, w))
            pooled = (s / cnt - ext_ref[POOL_STATE + t, :, sl]).astype(_CD)
            y_ref[t, :, sl] = (_dot(pooled, w_ref[g]) * sc_ref[:, sl]).astype(y_ref.dtype)


def _pool_sample(ext, w_pool, scale, Td, pos0):
    _, Bd, _ = ext.shape
    return pl.pallas_call(
        functools.partial(_pool_sample_body, Td=Td, pos0=pos0),
        out_shape=jax.ShapeDtypeStruct((Td, Bd, POOL_DIM), _CD),
        compiler_params=pltpu.CompilerParams(vmem_limit_bytes=VMEM_LIMIT), name="pool_sample",
    )(ext, w_pool, scale.reshape(1, POOL_DIM))


def _lanes(x, n):
    return x[:, :n] if n <= LANES else jnp.concatenate([x] * (n // LANES), axis=1)


def _fold_lanes(p):
    out = p[:, :LANES]
    for c in range(1, p.shape[1] // LANES):
        out = out + p[:, c * LANES:(c + 1) * LANES]
    return out


def _softmax_step(s, pv, m_ref, l_ref, acc_ref):
    m_prev = m_ref[...]
    m_new = jnp.maximum(m_prev, jnp.max(s, axis=-1, keepdims=True))
    alpha = jnp.exp(m_prev - m_new)
    p = jnp.exp(s - _lanes(m_new, s.shape[1]))
    l_ref[...] = alpha * l_ref[...] + _fold_lanes(p)
    acc_ref[...] = _lanes(alpha, acc_ref.shape[-1]) * acc_ref[...] + pv(p.astype(_CD))
    m_ref[...] = m_new


def _softmax_rescale(s, m_ref, l_ref, acc_ref, p_ref, first):
    m_new = jnp.max(s, axis=-1, keepdims=True)
    if first:
        m_new = jnp.broadcast_to(m_new, m_ref.shape)
    else:
        m_prev = m_ref[...]
        m_new = jnp.maximum(m_prev, m_new)
        alpha = jnp.exp(m_prev - m_new)
        acc_ref[...] = _lanes(alpha, acc_ref.shape[-1]) * acc_ref[...]
    p = jnp.exp(s - _lanes(m_new, s.shape[1]))
    if l_ref is not None:
        l_ref[...] = _fold_lanes(p) if first else alpha * l_ref[...] + _fold_lanes(p)
    p_ref[...] = p.astype(p_ref.dtype)
    m_ref[...] = m_new


def _mla_prompt_body(qn_ref, qr_ref, ckv_ref, kr_ref, wuk_ref, wuv_ref, o_ref, ql_s, qr_s, s_s, p_s, m_s, l_s, acc_s, *, TQ):
    i = pl.program_id(1)
    rows = [slice(h * TQ, (h + 1) * TQ) for h in range(MLA_HEADS)]
    for h in range(MLA_HEADS):
        ql_s[rows[h], :] = _dot(qn_ref[:, h * MLA_NOPE:(h + 1) * MLA_NOPE], wuk_ref[h]).astype(_CD)
        qr_s[rows[h], :] = qr_ref[:, h * MLA_ROPE:(h + 1) * MLA_ROPE]
    def block(off, visible, first):
        kc = ckv_ref[pl.ds(off, TQ), :]
        s_s[...] = (_dot_nt(ql_s[...], kc) + _dot_nt(qr_s[...], kr_ref[pl.ds(off, TQ), :])) * MLA_SCALE
        for r in rows:
            s = s_s[r, :]
            if visible is not None:
                s = jnp.where(visible, s, _NEG_INF)
            _softmax_rescale(s, m_s.at[r], l_s.at[r], acc_s.at[r], p_s.at[r], first)
        pv = _dot(p_s[...], kc)
        acc_s[...] = pv if first else acc_s[...] + pv

    tri = lax.broadcasted_iota(jnp.int32, (TQ, TQ), 1) <= lax.broadcasted_iota(jnp.int32, (TQ, TQ), 0)
    block(pl.multiple_of(i * TQ, TQ), tri, True)

    def body(j, c):
        block(pl.multiple_of(j * TQ, TQ), None, False)
        return c

    lax.fori_loop(0, i, body, 0)
    for h in range(MLA_HEADS):
        o_lat = (acc_s[rows[h], :] / jnp.sum(l_s[rows[h], :], axis=-1, keepdims=True)).astype(_CD)
        o_ref[:, h * MLA_V:(h + 1) * MLA_V] = _dot(o_lat, wuv_ref[h]).astype(o_ref.dtype)


def _mla_prompt(qn, qr, ckvb, krb, wuk, wuv, B, T):
    tq = _tile(T, ATTN_TILE)
    nq = T // tq
    R = MLA_HEADS * tq
    qmap = lambda b, i: (b * nq + i, 0)
    smap = lambda b, i: (b, 0)
    c3 = lambda b, i: (0, 0, 0)
    return pl.pallas_call(
        functools.partial(_mla_prompt_body, TQ=tq), grid=(B, nq),
        in_specs=[pl.BlockSpec((tq, qn.shape[1]), qmap), pl.BlockSpec((tq, qr.shape[1]), qmap),
                  pl.BlockSpec((T, MLA_KV_LORA), smap), pl.BlockSpec((T, MLA_ROPE), smap),
                  pl.BlockSpec(wuk.shape, c3), pl.BlockSpec(wuv.shape, c3)],
        out_specs=pl.BlockSpec((tq, MLA_HEADS * MLA_V), qmap),
        out_shape=jax.ShapeDtypeStruct((B * T, MLA_HEADS * MLA_V), _CD),
        scratch_shapes=[pltpu.VMEM((R, MLA_KV_LORA), _CD), pltpu.VMEM((R, MLA_ROPE), _CD),
                        pltpu.VMEM((R, tq), _F32), pltpu.VMEM((R, tq), _CD),
                        pltpu.VMEM((R, LANES), _F32), pltpu.VMEM((R, LANES), _F32), pltpu.VMEM((R, MLA_KV_LORA), _F32)],
        compiler_params=_cparams(2), name="mla_prompt",
    )(qn, qr, ckvb, krb, wuk, wuv)


def _mla_sample_body(pt_ref, qn_ref, qr_ref, nckv_ref, nkr_ref, wuk_ref, wuv_ref, *rest, PG, Td, TdP):
    ckv_pages, krt_pages = rest[:PG], rest[PG:2 * PG]
    o_ref, ql_s, qr_s, m_ref, l_ref, acc_ref = rest[2 * PG:]
    i = pl.program_id(1)
    R = MLA_HEADS * TdP

    @pl.when(i == 0)
    def _():
        for h in range(MLA_HEADS):
            ql_s[h * TdP:(h + 1) * TdP, :] = _dot(qn_ref[0][:, h * MLA_NOPE:(h + 1) * MLA_NOPE], wuk_ref[h])
            qr_s[h * TdP:(h + 1) * TdP, :] = qr_ref[0][:, h * MLA_ROPE:(h + 1) * MLA_ROPE].astype(_F32)
        m_ref[...] = jnp.full((R, LANES), _NEG_INF, _F32)
        l_ref[...] = jnp.zeros((R, LANES), _F32)
        acc_ref[...] = jnp.zeros((R, MLA_KV_LORA), _F32)

    ql = ql_s[...].astype(_CD)
    qr = qr_s[...].astype(_CD)
    kcs = [ckv_pages[j][...].astype(_CD) for j in range(PG)]
    s = jnp.concatenate([_dot_nt(ql, kcs[j]) + _dot(qr, krt_pages[j][...].astype(_CD)) for j in range(PG)], axis=1)

    def pv(p):
        out = _dot(p[:, :PAGE_SIZE], kcs[0])
        for j in range(1, PG):
            out = out + _dot(p[:, j * PAGE_SIZE:(j + 1) * PAGE_SIZE], kcs[j])
        return out

    _softmax_step(s * MLA_SCALE, pv, m_ref, l_ref, acc_ref)

    @pl.when(i == pl.num_programs(1) - 1)
    def _():
        kc = nckv_ref[0]
        s = (_dot_nt(ql, kc) + _dot_nt(qr, nkr_ref[0])) * MLA_SCALE
        t = lax.broadcasted_iota(jnp.int32, (R, PAGE_SIZE), 0) % TdP
        c = lax.broadcasted_iota(jnp.int32, (R, PAGE_SIZE), 1)
        s = jnp.where((c <= t) & (c < Td), s, _NEG_INF)
        _softmax_step(s, lambda p: _dot(p, kc), m_ref, l_ref, acc_ref)
        o_lat = (acc_ref[...] / jnp.sum(l_ref[...], axis=-1, keepdims=True)).astype(_CD)
        for h in range(MLA_HEADS):
            o_ref[0, :, h * MLA_V:(h + 1) * MLA_V] = _dot(o_lat[h * TdP:(h + 1) * TdP], wuv_ref[h])


def _page_spec(block, layer, PG, j):
    nd = len(block)

    def index_map(b, i, pt):
        return (layer, pt[b, i * PG + j]) + (0,) * (nd - 2)

    return pl.BlockSpec(block, index_map)


def _mla_sample(page_table, qn, qr, nckv, nkr, wuk, wuv, cache_ckv, cache_krt, layer, Td):
    Bd, TdP, _ = qn.shape
    n_pages = page_table.shape[1]
    pg = _tile(n_pages, PAGES_PER_STEP)
    seq = lambda b, i, pt: (b, 0, 0)
    c3 = lambda b, i, pt: (0, 0, 0)
    R = MLA_HEADS * TdP
    in_specs = [pl.BlockSpec((1, TdP, qn.shape[2]), seq), pl.BlockSpec((1, TdP, qr.shape[2]), seq),
                pl.BlockSpec((1, PAGE_SIZE, MLA_KV_LORA), seq), pl.BlockSpec((1, PAGE_SIZE, MLA_ROPE), seq),
                pl.BlockSpec(wuk.shape, c3), pl.BlockSpec(wuv.shape, c3)]
    in_specs += [_page_spec((None, None, PAGE_SIZE, MLA_KV_LORA), layer, pg, j) for j in range(pg)]
    in_specs += [_page_spec((None, None, MLA_ROPE, PAGE_SIZE), layer, pg, j) for j in range(pg)]
    return pl.pallas_call(
        functools.partial(_mla_sample_body, PG=pg, Td=Td, TdP=TdP),
        grid_spec=pltpu.PrefetchScalarGridSpec(
            num_scalar_prefetch=1, grid=(Bd, n_pages // pg), in_specs=in_specs,
            out_specs=pl.BlockSpec((1, TdP, MLA_HEADS * MLA_V), seq),
            scratch_shapes=[pltpu.VMEM((R, MLA_KV_LORA), _F32), pltpu.VMEM((R, MLA_ROPE), _F32),
                            pltpu.VMEM((R, LANES), _F32), pltpu.VMEM((R, LANES), _F32), pltpu.VMEM((R, MLA_KV_LORA), _F32)]),
        out_shape=jax.ShapeDtypeStruct((Bd, TdP, MLA_HEADS * MLA_V), _F32),
        compiler_params=_cparams(2), name="mla_sample",
    )(page_table, qn, qr, nckv, nkr, wuk, wuv, *([cache_ckv] * pg), *([cache_krt] * pg))


_O_Q, _O_K, _O_V, _O_QI, _O_KI, _O_WI, _O_END = 0, 1024, 1280, 1536, 2048, 2176, 2304
_KV = DSA_KV_HEADS * DSA_HEAD_DIM
assert all(math.frexp(s)[0] == 0.5 for s in (DSA_SCALE, IDX_SCALE))
_V1 = LANES


def _odd_proj_body(x_ref, g_ref, w_ref, q_ref, k_ref, v_ref, qi_ref, ki_ref, wi_ref, kb_ref, vb_ref, kib_ref, v1_ref, *, time_minor):
    h = _rms(x_ref[...], g_ref[...]).astype(_CD)
    z = _dot(h, w_ref[...])
    q_ref[...] = (z[:, _O_Q:_O_K] * DSA_SCALE).astype(_CD)
    k = z[:, _O_K:_O_V]
    v = z[:, _O_V:_O_QI]
    ki = z[:, _O_KI:_O_KI + IDX_DIM]
    if time_minor:
        k_ref[0], v_ref[0], ki_ref[0] = k.T, v.T, ki.T
    else:
        k_ref[...], v_ref[...], ki_ref[...] = k, v, ki
    qi_ref[...] = (z[:, _O_QI:_O_KI] * IDX_SCALE).astype(_CD)
    wi_ref[...] = z[:, _O_WI:_O_WI + IDX_HEADS] * (IDX_HEADS ** -0.5)
    kb_ref[...] = k.astype(_CD)
    vb_ref[...] = v.astype(_CD)
    kib_ref[...] = ki.astype(_CD)
    ones = jnp.ones((v.shape[0], _V1 - DSA_HEAD_DIM), _CD)
    for n in range(DSA_KV_HEADS):
        v1_ref[:, n * _V1:n * _V1 + DSA_HEAD_DIM] = v[:, n * DSA_HEAD_DIM:(n + 1) * DSA_HEAD_DIM].astype(_CD)
        v1_ref[:, n * _V1 + DSA_HEAD_DIM:(n + 1) * _V1] = ones


def _odd_proj(x, g, w_all, seq_len=None):
    n, d = x.shape
    tm = _tile(n, TOKEN_TILE)
    const = lambda i: (0, 0)
    row = lambda i: (i, 0)
    widths = (DSA_HEADS * DSA_HEAD_DIM, _KV, _KV, IDX_HEADS * IDX_DIM, IDX_DIM, IDX_HEADS, _KV, _KV, IDX_DIM,
              DSA_KV_HEADS * _V1)
    dtypes = (_CD, _F32, _F32, _CD, _F32, _F32, _CD, _CD, _CD, _CD)
    out_specs = [pl.BlockSpec((tm, w), row) for w in widths]
    out_shape = [jax.ShapeDtypeStruct((n, w), dt) for w, dt in zip(widths, dtypes)]
    if seq_len is not None:
        for o in (1, 2, 4):
            out_specs[o], out_shape[o] = _time_minor_out(n, tm, seq_len, widths[o])
    return pl.pallas_call(
        functools.partial(_odd_proj_body, time_minor=seq_len is not None), grid=(n // tm,),
        in_specs=[pl.BlockSpec((tm, d), row), pl.BlockSpec((1, d), const), pl.BlockSpec(w_all.shape, const)],
        out_specs=out_specs, out_shape=out_shape,
        compiler_params=_cparams(1), name="odd_proj",
    )(x, g.reshape(1, d), w_all)


def _kth_largest_search(count, topk, idx_bits, shape):
    def value_bit(b, t_u):
        cand_u = t_u | lax.shift_left(jnp.int32(1), 31 - b)
        cand = cand_u ^ _MIN_I32
        return jnp.where(count(lambda k, idx: k >= cand) >= topk, cand_u, t_u)

    thr = lax.fori_loop(0, 32, value_bit, jnp.zeros(shape, jnp.int32)) ^ _MIN_I32
    need = topk - count(lambda k, idx: k > thr)
    n_ge = count(lambda k, idx: k >= thr)

    def index_bit(b, cut):
        cand = cut | lax.shift_left(jnp.int32(1), idx_bits - 1 - b)
        return jnp.where(count(lambda k, idx: (k == thr) & (idx < cand)) < need, cand, cut)

    cut = lax.cond(jnp.max(n_ge) > topk,
                   lambda: lax.fori_loop(0, idx_bits, index_bit, jnp.zeros(shape, jnp.int32)),
                   lambda: jnp.full(shape, (1 << idx_bits) - 1, jnp.int32))
    return thr, cut


def _selected(kk, idx, thr, cut):
    return (kk > thr) | ((kk == thr) & (idx <= cut))


def _dsa_prompt_body(q_ref, qi_ref, wi_ref, k_ref, v1_ref, ki_ref, o_ref, key_s, bias_s, wib_s, qs, s_s, p_s, m_s, acc_s,
                     *, TQ, T, topk):
    i = pl.program_id(1)
    rowpos = i * TQ + lax.broadcasted_iota(jnp.int32, (TQ, 1), 0)
    col0 = lax.broadcasted_iota(jnp.int32, (1, TQ), 1)
    for h in range(IDX_HEADS):
        wib_s[h] = jnp.broadcast_to(wi_ref[:, h:h + 1], (TQ, LANES))

    def select(nch):
        blocks = [slice(j * TQ, (j + 1) * TQ) for j in range(nch)]
        if nch * TQ <= topk:
            assert nch == 1
            bias_s[:, blocks[0]] = jnp.where(col0 <= rowpos, 0.0, _MASKED)
            return
        for j, js in enumerate(blocks):
            kic = ki_ref[js, :]
            isc = jnp.zeros((TQ, TQ), _F32)
            for h in range(IDX_HEADS):
                s = _dot_nt(qi_ref[:, h * IDX_DIM:(h + 1) * IDX_DIM], kic)
                isc = isc + jnp.maximum(s, 0.0) * _lanes(wib_s[h], TQ)
            if j == nch - 1:
                isc = jnp.where(j * TQ + col0 <= rowpos, isc, _NEG_INF)
            key_s[:, js] = _sort_key(isc)

        def count(pred):
            acc = None
            for j, js in enumerate(blocks):
                part = _fold_lanes(jnp.where(pred(key_s[:, js], j * TQ + col0), 1.0, 0.0))
                acc = part if acc is None else acc + part
            return jnp.sum(acc, axis=1, keepdims=True)

        thr, cut = _kth_largest_search(count, float(topk), max(1, (T - 1).bit_length()), (TQ, 1))
        for j, js in enumerate(blocks):
            sel = _selected(key_s[:, js], j * TQ + col0, thr, cut)
            if j == nch - 1:
                sel = sel & (j * TQ + col0 <= rowpos)
            bias_s[:, js] = jnp.where(sel, 0.0, _MASKED)

    for c in range(T // TQ):
        pl.when(i == c)(functools.partial(select, c + 1))

    G = DSA_GROUP * TQ
    for hh in range(DSA_HEADS):
        qs[hh * TQ:(hh + 1) * TQ, :] = q_ref[:, hh * DSA_HEAD_DIM:(hh + 1) * DSA_HEAD_DIM]

    def block(off, first):
        def kv_head(n):
            g = slice(n * G, (n + 1) * G)
            s_s[g, :] = _dot_nt(qs[g, :], k_ref[pl.ds(off, TQ), n * DSA_HEAD_DIM:(n + 1) * DSA_HEAD_DIM])
            for hh in range(n * DSA_GROUP, (n + 1) * DSA_GROUP):
                r = slice(hh * TQ, (hh + 1) * TQ)
                _softmax_rescale(s_s[r, :] + bias_s[:, pl.ds(off, TQ)], m_s.at[r], None, acc_s.at[r], p_s.at[r], first)
            pv = _dot(p_s[g, :], v1_ref[pl.ds(off, TQ), n * _V1:(n + 1) * _V1])
            acc_s[g, :] = pv if first else acc_s[g, :] + pv

        for n in range(DSA_KV_HEADS):
            pl.when(i >= 0)(functools.partial(kv_head, n))

    block(0, True)

    def body(j, c):
        block(pl.multiple_of(j * TQ, TQ), False)
        return c

    lax.fori_loop(1, i + 1, body, 0)
    for hh in range(DSA_HEADS):
        a = acc_s[hh * TQ:(hh + 1) * TQ, :]
        o_ref[:, hh * DSA_HEAD_DIM:(hh + 1) * DSA_HEAD_DIM] = (a[:, :DSA_HEAD_DIM] / a[:, _V1 - DSA_HEAD_DIM:]).astype(o_ref.dtype)


def _dsa_prompt(q, qi, wi, kb, v1, kib, B, T):
    tq = _tile(T, ATTN_TILE)
    nq = T // tq
    topk = min(TOPK_MAX, T // 4)
    qmap = lambda b, i: (b * nq + i, 0)
    smap = lambda b, i: (b, 0)
    return pl.pallas_call(
        functools.partial(_dsa_prompt_body, TQ=tq, T=T, topk=topk), grid=(B, nq),
        in_specs=[pl.BlockSpec((tq, q.shape[1]), qmap), pl.BlockSpec((tq, qi.shape[1]), qmap),
                  pl.BlockSpec((tq, IDX_HEADS), qmap),
                  pl.BlockSpec((T, _KV), smap), pl.BlockSpec((T, v1.shape[1]), smap), pl.BlockSpec((T, IDX_DIM), smap)],
        out_specs=pl.BlockSpec((tq, DSA_HEADS * DSA_HEAD_DIM), qmap),
        out_shape=jax.ShapeDtypeStruct((B * T, DSA_HEADS * DSA_HEAD_DIM), _CD),
        scratch_shapes=[pltpu.VMEM((tq, T), jnp.int32), pltpu.VMEM((tq, T), _F32),
                        pltpu.VMEM((IDX_HEADS, tq, LANES), _F32), pltpu.VMEM((DSA_HEADS * tq, DSA_HEAD_DIM), _CD),
                        pltpu.VMEM((DSA_HEADS * tq, tq), _F32), pltpu.VMEM((DSA_HEADS * tq, tq), _CD),
                        pltpu.VMEM((DSA_HEADS * tq, LANES), _F32), pltpu.VMEM((DSA_HEADS * tq, _V1), _F32)],
        compiler_params=_cparams(2), name="dsa_prompt",
    )(q, qi, wi, kb, v1, kib)


def _dsa_keys_body(pt_ref, qi_ref, wi_ref, nki_ref, *rest, PG, P, Td, TdP):
    kit_pages = rest[:PG]
    key_ref, qs, ws = rest[PG:]
    i = pl.program_id(1)

    @pl.when(i == 0)
    def _():
        for h in range(IDX_HEADS):
            qs[h * TdP:(h + 1) * TdP, :] = qi_ref[0][:, h * IDX_DIM:(h + 1) * IDX_DIM].astype(_F32)
            ws[h * TdP:(h + 1) * TdP, :] = jnp.broadcast_to(wi_ref[0][:, h:h + 1], (TdP, PAGE_SIZE))

    def index_scores(qk):
        s = jnp.maximum(qk, 0.0) * ws[...]
        isc = s[0:TdP]
        for h in range(1, IDX_HEADS):
            isc = isc + s[h * TdP:(h + 1) * TdP]
        return isc

    qb = qs[...].astype(_CD)
    for j in range(PG):
        off = pl.multiple_of((i * PG + j) * PAGE_SIZE, PAGE_SIZE)
        key_ref[0, :, pl.ds(off, PAGE_SIZE)] = _sort_key(index_scores(_dot(qb, kit_pages[j][...].astype(_CD))))

    @pl.when(i == pl.num_programs(1) - 1)
    def _():
        isc = index_scores(_dot_nt(qb, nki_ref[0]))
        t = lax.broadcasted_iota(jnp.int32, (TdP, PAGE_SIZE), 0)
        c = lax.broadcasted_iota(jnp.int32, (TdP, PAGE_SIZE), 1)
        key_ref[0, :, P:P + PAGE_SIZE] = _sort_key(jnp.where((c <= t) & (c < Td), isc, _NEG_INF))


def _dsa_keys(page_table, qi, wi, nki, cache_kit, layer, Td):
    Bd, TdP, _ = qi.shape
    n_pages = page_table.shape[1]
    P = n_pages * PAGE_SIZE
    pg = _tile(n_pages, PAGES_PER_STEP)
    seq = lambda b, i, pt: (b, 0, 0)
    R = IDX_HEADS * TdP
    in_specs = [pl.BlockSpec((1, TdP, qi.shape[2]), seq), pl.BlockSpec((1, TdP, IDX_HEADS), seq),
                pl.BlockSpec((1, PAGE_SIZE, IDX_DIM), seq)]
    in_specs += [_page_spec((None, None, IDX_DIM, PAGE_SIZE), layer, pg, j) for j in range(pg)]
    return pl.pallas_call(
        functools.partial(_dsa_keys_body, PG=pg, P=P, Td=Td, TdP=TdP),
        grid_spec=pltpu.PrefetchScalarGridSpec(
            num_scalar_prefetch=1, grid=(Bd, n_pages // pg), in_specs=in_specs,
            out_specs=pl.BlockSpec((1, TdP, P + PAGE_SIZE), seq),
            scratch_shapes=[pltpu.VMEM((R, IDX_DIM), _F32), pltpu.VMEM((R, PAGE_SIZE), _F32)]),
        out_shape=jax.ShapeDtypeStruct((Bd, TdP, P + PAGE_SIZE), jnp.int32),
        compiler_params=_cparams(2), name="dsa_keys",
    )(page_table, qi, wi, nki, *([cache_kit] * pg))


def _dsa_threshold_body(key_ref, thr_ref, cut_ref, *, topk):
    rows, width = key_ref.shape
    idx = lax.broadcasted_iota(jnp.int32, (1, width), 1)

    def count(pred):
        hit = jnp.where(pred(key_ref[...], idx), 1.0, 0.0)
        parts = [hit[:, c * LANES:(c + 1) * LANES] for c in range(width // LANES)]
        while len(parts) > 1:
            parts = [a + b for a, b in zip(parts[::2], parts[1::2])] + ([parts[-1]] if len(parts) % 2 else [])
        return jnp.sum(parts[0], axis=1, keepdims=True)

    thr, cut = _kth_largest_search(count, float(topk), (width - 1).bit_length(), (rows, 1))
    thr_ref[...] = jnp.broadcast_to(thr, (rows, LANES))
    cut_ref[...] = jnp.broadcast_to(cut, (rows, LANES))


def _dsa_threshold(keys, topk):
    rows, width = keys.shape
    rb = _tile(rows, SEARCH_ROWS)
    row = lambda i: (i, 0)
    return pl.pallas_call(
        functools.partial(_dsa_threshold_body, topk=topk), grid=(rows // rb,),
        in_specs=[pl.BlockSpec((rb, width), row)],
        out_specs=[pl.BlockSpec((rb, LANES), row), pl.BlockSpec((rb, LANES), row)],
        out_shape=[jax.ShapeDtypeStruct((rows, LANES), jnp.int32)] * 2,
        compiler_params=_cparams(1), name="dsa_threshold",
    )(keys)


def _dsa_sample_body(pt_ref, q_ref, key_ref, nkey_ref, thr_ref, cut_ref, nk_ref, nv_ref, *rest, PG, P, Td, TdP):
    kt_pages, vt_pages = rest[:PG], rest[PG:2 * PG]
    o_ref, qs, m_ref, l_ref, acc_ref = rest[2 * PG:]
    i = pl.program_id(1)
    RH = DSA_GROUP * TdP
    R = DSA_KV_HEADS * RH
    heads = [slice(n * RH, (n + 1) * RH) for n in range(DSA_KV_HEADS)]

    @pl.when(i == 0)
    def _():
        for hh in range(DSA_HEADS):
            qs[hh * TdP:(hh + 1) * TdP, :] = q_ref[0][:, hh * DSA_HEAD_DIM:(hh + 1) * DSA_HEAD_DIM].astype(_F32)
        m_ref[...] = jnp.full((R, LANES), _MASKED, _F32)
        l_ref[...] = jnp.zeros((R, LANES), _F32)
        acc_ref[...] = jnp.zeros((R, DSA_HEAD_DIM), _F32)

    thr = thr_ref[0][:, 0:1]
    cut = cut_ref[0][:, 0:1]
    qb = qs[...].astype(_CD)

    def attend(kk, idx, visible, scores, pv):
        sel = _selected(kk, idx, thr, cut)
        if visible is not None:
            sel = sel & visible
        bias = jnp.concatenate([jnp.where(sel, 0.0, _MASKED)] * (R // TdP), axis=0)
        s = jnp.concatenate([scores(n, qb[heads[n]]) for n in range(DSA_KV_HEADS)], axis=0) + bias
        _softmax_step(s, lambda p: jnp.concatenate([pv(n, p[heads[n]]) for n in range(DSA_KV_HEADS)], axis=0),
                      m_ref, l_ref, acc_ref)

    kts = [kt_pages[j][...].astype(_CD) for j in range(PG)]
    vts = [vt_pages[j][...].astype(_CD) for j in range(PG)]
    W = PG * PAGE_SIZE
    idx = i * W + lax.broadcasted_iota(jnp.int32, (TdP, W), 1)

    def past_pv(n, p):
        out = _dot_nt(p[:, :PAGE_SIZE], vts[0][n])
        for j in range(1, PG):
            out = out + _dot_nt(p[:, j * PAGE_SIZE:(j + 1) * PAGE_SIZE], vts[j][n])
        return out

    attend(key_ref[0], idx, None,
           lambda n, q: jnp.concatenate([_dot(q, kts[j][n]) for j in range(PG)], axis=1), past_pv)

    @pl.when(i == pl.num_programs(1) - 1)
    def _():
        t = lax.broadcasted_iota(jnp.int32, (TdP, PAGE_SIZE), 0)
        c = lax.broadcasted_iota(jnp.int32, (TdP, PAGE_SIZE), 1)
        attend(nkey_ref[0], P + c, (c <= t) & (c < Td),
               lambda n, q: _dot_nt(q, nk_ref[0][:, n * DSA_HEAD_DIM:(n + 1) * DSA_HEAD_DIM]),
               lambda n, p: _dot(p, nv_ref[0][:, n * DSA_HEAD_DIM:(n + 1) * DSA_HEAD_DIM]))
        o = acc_ref[...] / jnp.sum(l_ref[...], axis=-1, keepdims=True)
        for hh in range(DSA_HEADS):
            o_ref[0, :, hh * DSA_HEAD_DIM:(hh + 1) * DSA_HEAD_DIM] = o[hh * TdP:(hh + 1) * TdP]


def _dsa_sample(page_table, q, keys, nkeys, thr, cut, nk, nv, cache_kt, cache_vt, layer, Td):
    Bd, TdP, _ = q.shape
    n_pages = page_table.shape[1]
    P = n_pages * PAGE_SIZE
    pg = _tile(n_pages, PAGES_PER_STEP)
    seq = lambda b, i, pt: (b, 0, 0)
    R = DSA_HEADS * TdP
    in_specs = [pl.BlockSpec((1, TdP, q.shape[2]), seq),
                pl.BlockSpec((1, TdP, pg * PAGE_SIZE), lambda b, i, pt: (b, 0, i)),
                pl.BlockSpec((1, TdP, PAGE_SIZE), seq), pl.BlockSpec((1, TdP, LANES), seq),
                pl.BlockSpec((1, TdP, LANES), seq),
                pl.BlockSpec((1, PAGE_SIZE, _KV), seq), pl.BlockSpec((1, PAGE_SIZE, _KV), seq)]
    page_block = (None, None, DSA_KV_HEADS, DSA_HEAD_DIM, PAGE_SIZE)
    in_specs += [_page_spec(page_block, layer, pg, j) for j in range(pg)]
    in_specs += [_page_spec(page_block, layer, pg, j) for j in range(pg)]
    return pl.pallas_call(
        functools.partial(_dsa_sample_body, PG=pg, P=P, Td=Td, TdP=TdP),
        grid_spec=pltpu.PrefetchScalarGridSpec(
            num_scalar_prefetch=1, grid=(Bd, n_pages // pg), in_specs=in_specs,
            out_specs=pl.BlockSpec((1, TdP, DSA_HEADS * DSA_HEAD_DIM), seq),
            scratch_shapes=[pltpu.VMEM((R, DSA_HEAD_DIM), _F32), pltpu.VMEM((R, LANES), _F32),
                            pltpu.VMEM((R, LANES), _F32), pltpu.VMEM((R, DSA_HEAD_DIM), _F32)]),
        out_shape=jax.ShapeDtypeStruct((Bd, TdP, DSA_HEADS * DSA_HEAD_DIM), _F32),
        compiler_params=_cparams(2), name="dsa_sample",
    )(page_table, q, keys, nkeys, thr, cut, nk, nv, *([cache_kt] * pg), *([cache_vt] * pg))


def _rope_tables(pos):
    half = MLA_ROPE // 2
    inv = ROPE_THETA ** (-jnp.arange(half, dtype=_F32) / half)
    ang = pos.astype(_F32)[:, None] * inv[None, :]
    cos, sin = jnp.cos(ang), jnp.sin(ang)
    return (jnp.tile(jnp.concatenate([cos, cos], axis=1), (1, MLA_HEADS)),
            jnp.tile(jnp.concatenate([-sin, sin], axis=1), (1, MLA_HEADS)))


def _pack_cols(w, pieces, total):
    out = jnp.zeros((w.shape[0], total), w.dtype)
    for start, cols in pieces:
        out = out.at[:, start:start + len(cols)].set(w[:, np.asarray(cols)])
    return out.astype(_CD)


def _swap_halves(n):
    return (np.arange(n) + n // 2) % n


def _pad_rows(a, rows):
    return jnp.pad(a, ((0, 0), (0, rows - a.shape[1]), (0, 0)))


def kernel(x_prompt, x_sample, cache_mla_ckv, cache_mla_kr, cache_dsa_k, cache_dsa_v, cache_dsa_kidx, state_pool, page_table, ffn_norm, ffn_w1, ffn_w3, ffn_w2, mix_norm, final_norm, even_w_in, mla_q_norm, mla_kv_norm, mla_w_uq, mla_w_uk, mla_w_uv, pool_w, pool_scale, even_w_out, odd_w_in, odd_w_out):
    B, T, D = x_prompt.shape
    Bd, Td, _ = x_sample.shape
    depth = ffn_norm.shape[0]
    P = page_table.shape[1] * PAGE_SIZE
    TdP = -(-Td // SUBLANES) * SUBLANES
    assert D == D_MODEL and Td <= PAGE_SIZE and T >= POOL_STATE
    Np, Ns = B * T, Bd * Td
    tm_s = _tile(Ns, TOKEN_TILE)
    assert tm_s % Td == 0

    pos_p = jnp.arange(T, dtype=jnp.int32)
    pos_s = P + jnp.arange(Td, dtype=jnp.int32)
    cos_p, sin_p = _rope_tables(pos_p)
    cos_s, sin_s = (jnp.tile(a, (tm_s // Td, 1)) for a in _rope_tables(pos_s))

    def seq_rows(a, rows=TdP):
        return _pad_rows(a.reshape(Bd, Td, a.shape[1]), rows)

    cache_krt = jnp.swapaxes(cache_mla_kr, 2, 3)
    cache_kit = jnp.swapaxes(cache_dsa_kidx, 2, 3)
    cache_kt = jnp.transpose(cache_dsa_k, (0, 1, 3, 4, 2))
    cache_vt = jnp.transpose(cache_dsa_v, (0, 1, 3, 4, 2))

    xp = x_prompt.reshape(Np, D)
    xs = x_sample.reshape(Ns, D)
    pre_p, pre_s = (), ()
    outs = {k: [] for k in ("p_ckv", "p_kr", "p_k", "p_v", "p_ki", "p_pool", "s_ckv", "s_kr", "s_k", "s_v", "s_ki", "s_pool")}
    ffn_w = [w.astype(_CD) for w in (ffn_w1, ffn_w3, ffn_w2)]
    for l in range(depth):
        j = l // 2
        wa = [w[l, 0] for w in ffn_w]
        wb = [w[l, 1] for w in ffn_w]
        xp = _ffn(xp, ffn_norm[l, 0], *wa, pre=pre_p)
        xs = _ffn(xs, ffn_norm[l, 0], *wa, pre=pre_s)
        if l % 2 == 0:
            kr0 = POOL_DIM + MLA_Q_LORA + MLA_KV_LORA
            w_all = _pack_cols(even_w_in[j], [(_E_U, np.arange(kr0)), (_E_KR, kr0 + np.arange(MLA_ROPE)),
                                              (_E_KRS, kr0 + _swap_halves(MLA_ROPE))], _E_END)
            per_head = MLA_NOPE + MLA_ROPE
            heads = np.arange(MLA_HEADS)[:, None] * per_head
            wuq_all = _pack_cols(mla_w_uq[j], [(_Q_NOPE, (heads + np.arange(MLA_NOPE)[None]).ravel()),
                                               (_Q_ROPE, (heads + MLA_NOPE + np.arange(MLA_ROPE)[None]).ravel()),
                                               (_Q_ROPES, (heads + MLA_NOPE + _swap_halves(MLA_ROPE)[None]).ravel())], _Q_END)
            wuk = jnp.transpose(mla_w_uk[j], (1, 2, 0)).astype(_CD)
            wuv = jnp.transpose(mla_w_uv[j], (1, 0, 2)).astype(_CD)
            w_out = even_w_out[j].astype(_CD)
            pw = pool_w[j].astype(_CD)
            u, qn, qr, ckv, kr, ckvb, krb = _even_proj(xp, mix_norm[l], w_all, mla_q_norm[j], mla_kv_norm[j], wuq_all, cos_p, sin_p, T)
            y_pool = _pool_prompt(u, pw, pool_scale[j], B, T)
            o = _mla_prompt(qn, qr, ckvb, krb, wuk, wuv, B, T)
            pre_p = ((y_pool, w_out[:POOL_DIM]), (o, w_out[POOL_DIM:]))
            outs["p_ckv"].append(ckv.reshape(B, T, MLA_KV_LORA))
            outs["p_kr"].append(jnp.swapaxes(kr, 1, 2))
            outs["p_pool"].append(u.reshape(B, T, POOL_DIM)[:, T - POOL_STATE:])
            u, qn, qr, ckv, kr, ckvb, krb = _even_proj(xs, mix_norm[l], w_all, mla_q_norm[j], mla_kv_norm[j], wuq_all, cos_s, sin_s)
            ext = jnp.concatenate([state_pool[j], u.reshape(Bd, Td, POOL_DIM)], axis=1)
            y_pool = _pool_sample(jnp.swapaxes(ext, 0, 1), pw, pool_scale[j], Td, P)
            y_pool = jnp.swapaxes(y_pool, 0, 1).reshape(Ns, POOL_DIM)
            o = _mla_sample(page_table, seq_rows(qn), seq_rows(qr), seq_rows(ckvb, PAGE_SIZE), seq_rows(krb, PAGE_SIZE),
                            wuk, wuv, cache_mla_ckv, cache_krt, j, Td)
            o = o[:, :Td].reshape(Ns, MLA_HEADS * MLA_V).astype(_CD)
            pre_s = ((y_pool, w_out[:POOL_DIM]), (o, w_out[POOL_DIM:]))
            outs["s_ckv"].append(ckv.reshape(Bd, Td, MLA_KV_LORA))
            outs["s_kr"].append(kr.reshape(Bd, Td, MLA_ROPE))
            outs["s_pool"].append(ext[:, -POOL_STATE:])
        else:
            o5 = (DSA_HEADS + 2 * DSA_KV_HEADS) * DSA_HEAD_DIM + IDX_HEADS * IDX_DIM + IDX_DIM
            o4 = o5 - IDX_DIM
            w_all = _pack_cols(odd_w_in[j], [(_O_Q, np.arange(o4)), (_O_KI, o4 + np.arange(IDX_DIM)),
                                             (_O_WI, o5 + np.arange(IDX_HEADS))], _O_END)
            w_out = odd_w_out[j].astype(_CD)
            q, k, v, qi, ki, wi, kb, vb, kib, v1 = _odd_proj(xp, mix_norm[l], w_all, T)
            o = _dsa_prompt(q, qi, wi, kb, v1, kib, B, T)
            pre_p = ((o, w_out),)
            per_head = lambda a: jnp.transpose(a.reshape(B, DSA_KV_HEADS, DSA_HEAD_DIM, T), (0, 3, 1, 2))
            outs["p_k"].append(per_head(k))
            outs["p_v"].append(per_head(v))
            outs["p_ki"].append(jnp.swapaxes(ki, 1, 2))
            q, k, v, qi, ki, wi, kb, vb, kib, _ = _odd_proj(xs, mix_norm[l], w_all)
            keys = _dsa_keys(page_table, seq_rows(qi), seq_rows(wi), seq_rows(kib, PAGE_SIZE), cache_kit, j, Td)
            thr, cut = _dsa_threshold(keys[:, :Td].reshape(Ns, P + PAGE_SIZE), min(TOPK_MAX, (P + Td) // 4))
            o = _dsa_sample(page_table, seq_rows(q), keys, keys[:, :, P:], seq_rows(thr), seq_rows(cut), seq_rows(kb, PAGE_SIZE),
                            seq_rows(vb, PAGE_SIZE), cache_kt, cache_vt, j, Td)
            o = o[:, :Td].reshape(Ns, DSA_HEADS * DSA_HEAD_DIM).astype(_CD)
            pre_s = ((o, w_out),)
            outs["s_k"].append(k.reshape(Bd, Td, DSA_KV_HEADS, DSA_HEAD_DIM))
            outs["s_v"].append(v.reshape(Bd, Td, DSA_KV_HEADS, DSA_HEAD_DIM))
            outs["s_ki"].append(ki.reshape(Bd, Td, IDX_DIM))
        last = l == depth - 1
        xp = _ffn(xp, ffn_norm[l, 1], *wb, pre=pre_p, final_g=final_norm if last else None)
        xs = _ffn(xs, ffn_norm[l, 1], *wb, pre=pre_s, final_g=final_norm if last else None)
        pre_p, pre_s = (), ()
    st = {k: jnp.stack(v) for k, v in outs.items()}
    return (xp.reshape(B, T, D), xs.reshape(Bd, Td, D),
            st["p_ckv"], st["p_kr"], st["p_k"], st["p_v"], st["p_ki"], st["p_pool"],
            st["s_ckv"], st["s_kr"], st["s_k"], st["s_v"], st["s_ki"], st["s_pool"])
```
